```python
import jax, jax.numpy as jnp
from jax import lax
import numpy as np

D_MODEL = 4096
BATCH = 2
SEQ = 4096
DEPTH = 2

MIX_WIDTH = D_MODEL
GROUP_WIDTH = MIX_WIDTH // 4
GLA_HEADS = 4
GLA_DV = GROUP_WIDTH // GLA_HEADS
GLA_DK = GLA_DV // 2
GLA_GATE_RANK = 16
GLA_GATE_TAU = 16.0
GLA_CHUNK = 64
SB_HEADS = 8
SB_HEAD_DIM = GROUP_WIDTH // SB_HEADS
SB_BLOCK = 128
LRU_WIDTH = GROUP_WIDTH
LRU_BLOCKS = 8
LRU_BLOCK_DIM = LRU_WIDTH // LRU_BLOCKS
LRU_CONV = 4
LRU_C = 8.0
CONV_WIDTH = GROUP_WIDTH
CONV_KERNEL = 31
FFN_HIDDEN = ((8 * D_MODEL + 3 * 256 - 1) // (3 * 256)) * 256

IN_SIZES = (
    GLA_HEADS * GLA_DK,
    GLA_HEADS * GLA_DK,
    GROUP_WIDTH,
    GROUP_WIDTH,
    GLA_GATE_RANK,
    GROUP_WIDTH,
    GROUP_WIDTH,
    GROUP_WIDTH,
    LRU_WIDTH,
    LRU_WIDTH,
    2 * CONV_WIDTH,
)
N_IN = sum(IN_SIZES)

kernel_name = "hybrid_gla_stickbreak_rglru_conformer_block"


def rmsnorm(x, g, eps=1e-6):
    xf = x.astype(jnp.float32)
    y = xf * lax.rsqrt(jnp.mean(xf * xf, axis=-1, keepdims=True) + eps)
    return (y * g.astype(jnp.float32)).astype(x.dtype)


def layernorm(x, g, b, eps=1e-5):
    xf = x.astype(jnp.float32)
    mu = jnp.mean(xf, axis=-1, keepdims=True)
    var = jnp.mean(jnp.square(xf - mu), axis=-1, keepdims=True)
    y = (xf - mu) * lax.rsqrt(var + eps)
    return (y * g.astype(jnp.float32) + b.astype(jnp.float32)).astype(x.dtype)


def causal_depthwise_conv(x, w, b):
    k, c = w.shape
    y = lax.conv_general_dilated(
        x, w[:, None, :].astype(x.dtype), window_strides=(1,),
        padding=[(k - 1, 0)], dimension_numbers=('NWC', 'WIO', 'NWC'),
        feature_group_count=c)
    return y + b


def split_columns(z, sizes):
    offs = np.cumsum(sizes)[:-1].tolist()
    return jnp.split(z, offs, axis=-1)


def gla_mixer(q, k, v, g_out, gate_lr, w_gate2, b_gate, norm_g):
    bsz, t, _ = q.shape
    n = t // GLA_CHUNK
    f32 = jnp.float32

    def chunks(z, d):
        return z.astype(f32).reshape(bsz, n, GLA_CHUNK, GLA_HEADS, d).transpose(1, 0, 3, 2, 4)

    log_alpha = jax.nn.log_sigmoid((gate_lr @ w_gate2 + b_gate).astype(f32)) / GLA_GATE_TAU
    qc = chunks(q, GLA_DK) * (GLA_DK ** -0.5)
    kc = chunks(k, GLA_DK)
    vc = chunks(v, GLA_DV)
    gc = chunks(log_alpha, GLA_DK)
    causal = jnp.tril(jnp.ones((GLA_CHUNK, GLA_CHUNK), dtype=bool))

    def step(state, inp):
        qb, kb, vb, gb = inp
        cum = jnp.cumsum(gb, axis=2)
        o_inter = jnp.einsum('bhid,bhde->bhie', qb * jnp.exp(cum), state)
        diff = cum[:, :, :, None, :] - cum[:, :, None, :, :]
        decay = jnp.where(causal[:, :, None], jnp.exp(jnp.minimum(diff, 0.0)), 0.0)
        scores = jnp.einsum('bhid,bhjd,bhijd->bhij', qb, kb, decay)
        o_intra = jnp.einsum('bhij,bhje->bhie', scores, vb)
        last = cum[:, :, -1:, :]
        k_dec = kb * jnp.exp(last - cum)
        new_state = jnp.exp(last[:, :, 0, :])[..., None] * state + jnp.einsum('bhjd,bhje->bhde', k_dec, vb)
        return new_state, o_inter + o_intra

    s0 = jnp.zeros((bsz, GLA_HEADS, GLA_DK, GLA_DV), f32)
    _, outs = lax.scan(step, s0, (qc, kc, vc, gc))
    o = outs.transpose(1, 0, 3, 2, 4).reshape(bsz, t, GLA_HEADS, GLA_DV)
    o = rmsnorm(o, norm_g).reshape(bsz, t, GROUP_WIDTH)
    return (o * jax.nn.silu(g_out.astype(f32))).astype(q.dtype)


def stick_breaking_mixer(q, k, v):
    bsz, t, _ = q.shape
    f32 = jnp.float32

    def heads(z):
        return z.reshape(bsz, t, SB_HEADS, SB_HEAD_DIM).transpose(0, 2, 1, 3)

    qh, kh, vh = heads(q), heads(k), heads(v)
    scale = SB_HEAD_DIM ** -0.5
    outs = []
    for blk in range(t // SB_BLOCK):
        q0 = blk * SB_BLOCK
        q1 = q0 + SB_BLOCK
        qb = qh[:, :, q0:q1]
        kb = kh[:, :, :q1]
        vb = vh[:, :, :q1]
        z = jnp.einsum('bhtd,bhsd->bhts', qb, kb).astype(f32) * scale
        before = jnp.arange(q1)[None, :] < jnp.arange(q0, q1)[:, None]
        log_beta = jax.nn.log_sigmoid(z)
        log_rest = jnp.where(before, log_beta - z, 0.0)
        tail = lax.cumsum(log_rest, axis=3, reverse=True) - log_rest
        w = jnp.where(before, jnp.exp(log_beta + tail), 0.0)
        outs.append(jnp.einsum('bhts,bhsd->bhtd', w.astype(vb.dtype), vb))
    o = jnp.concatenate(outs, axis=2)
    return o.transpose(0, 2, 1, 3).reshape(bsz, t, GROUP_WIDTH)


def rglru_mixer(xb, gate_b, conv_w, conv_b, w_a, b_a, w_x, b_x, lam):
    bsz, t, _ = xb.shape
    f32 = jnp.float32
    xc = causal_depthwise_conv(xb, conv_w, conv_b)
    xblk = xc.reshape(bsz, t, LRU_BLOCKS, LRU_BLOCK_DIM)
    r = jax.nn.sigmoid(jnp.einsum('btnd,nde->btne', xblk, w_a).reshape(bsz, t, LRU_WIDTH) + b_a)
    i = jax.nn.sigmoid(jnp.einsum('btnd,nde->btne', xblk, w_x).reshape(bsz, t, LRU_WIDTH) + b_x)
    log_a = -LRU_C * r.astype(f32) * jax.nn.softplus(-lam.astype(f32))
    a = jnp.exp(log_a)
    u = jnp.sqrt(-jnp.expm1(2.0 * log_a)) * (i * xc).astype(f32)

    def combine(left, right):
        a_l, u_l = left
        a_r, u_r = right
        return a_l * a_r, a_r * u_l + u_r

    _, h = lax.associative_scan(combine, (a, u), axis=1)
    return h.astype(xb.dtype) * jax.nn.gelu(gate_b)


def conformer_conv_mixer(u, conv_w, conv_b, ln_g, ln_b):
    val, gte = jnp.split(u, 2, axis=-1)
    y = causal_depthwise_conv(val * jax.nn.sigmoid(gte), conv_w, conv_b)
    return jax.nn.silu(layernorm(y, ln_g, ln_b))


def setup_inputs(seed: int = 0) -> dict:
    key = jax.random.key(seed)
    ks = jax.random.split(key, 24)
    f32 = jnp.float32
    nrm = lambda k, shape, s: jax.random.normal(k, shape, f32) * s
    gain = lambda k, shape: 1.0 + 0.01 * jax.random.normal(k, shape, f32)
    a_init = jax.random.uniform(ks[12], (DEPTH, LRU_WIDTH), f32, 0.9, 0.999)
    return {
        "x": jax.random.normal(ks[0], (BATCH, SEQ, D_MODEL), f32),
        "norm_mix_g": gain(ks[1], (DEPTH, D_MODEL)),
        "w_in": nrm(ks[2], (DEPTH, D_MODEL, N_IN), D_MODEL ** -0.5),
        "gla_w_gate2": nrm(ks[3], (DEPTH, GLA_GATE_RANK, GLA_HEADS * GLA_DK), GLA_GATE_RANK ** -0.5),
        "gla_b_gate": nrm(ks[4], (DEPTH, GLA_HEADS * GLA_DK), 0.01),
        "gla_norm_g": gain(ks[5], (DEPTH, GLA_DV)),
        "lru_conv_w": nrm(ks[6], (DEPTH, LRU_CONV, LRU_WIDTH), LRU_CONV ** -0.5),
        "lru_conv_b": nrm(ks[7], (DEPTH, LRU_WIDTH), 0.01),
        "lru_w_a": nrm(ks[8], (DEPTH, LRU_BLOCKS, LRU_BLOCK_DIM, LRU_BLOCK_DIM), LRU_BLOCK_DIM ** -0.5),
        "lru_b_a": nrm(ks[9], (DEPTH, LRU_WIDTH), 0.01),
        "lru_w_x": nrm(ks[10], (DEPTH, LRU_BLOCKS, LRU_BLOCK_DIM, LRU_BLOCK_DIM), LRU_BLOCK_DIM ** -0.5),
        "lru_b_x": nrm(ks[11], (DEPTH, LRU_WIDTH), 0.01),
        "lru_lambda": jnp.log(a_init) - jnp.log1p(-a_init),
        "conf_conv_w": nrm(ks[13], (DEPTH, CONV_KERNEL, CONV_WIDTH), CONV_KERNEL ** -0.5),
        "conf_conv_b": nrm(ks[14], (DEPTH, CONV_WIDTH), 0.01),
        "conf_ln_g": gain(ks[15], (DEPTH, CONV_WIDTH)),
        "conf_ln_b": nrm(ks[16], (DEPTH, CONV_WIDTH), 0.01),
        "w_out": nrm(ks[17], (DEPTH, MIX_WIDTH, D_MODEL), MIX_WIDTH ** -0.5),
        "norm_ffn_g": gain(ks[18], (DEPTH, D_MODEL)),
        "ffn_w_gate": nrm(ks[19], (DEPTH, D_MODEL, FFN_HIDDEN), D_MODEL ** -0.5),
        "ffn_w_up": nrm(ks[20], (DEPTH, D_MODEL, FFN_HIDDEN), D_MODEL ** -0.5),
        "ffn_w_down": nrm(ks[21], (DEPTH, FFN_HIDDEN, D_MODEL), FFN_HIDDEN ** -0.5),
        "final_norm_g": gain(ks[22], (D_MODEL,)),
    }


def reference(x, norm_mix_g, w_in, gla_w_gate2, gla_b_gate, gla_norm_g,
              lru_conv_w, lru_conv_b, lru_w_a, lru_b_a, lru_w_x, lru_b_x, lru_lambda,
              conf_conv_w, conf_conv_b, conf_ln_g, conf_ln_b,
              w_out, norm_ffn_g, ffn_w_gate, ffn_w_up, ffn_w_down, final_norm_g):
    for l in range(DEPTH):
        h = rmsnorm(x, norm_mix_g[l])
        proj = h @ w_in[l]
        (gq, gk, gv, gg, glr, sq, sk, sv, lx, lg, cu) = split_columns(proj, IN_SIZES)
        o_a = gla_mixer(gq, gk, gv, gg, glr, gla_w_gate2[l], gla_b_gate[l], gla_norm_g[l])
        o_b = stick_breaking_mixer(sq, sk, sv)
        o_c = rglru_mixer(lx, lg, lru_conv_w[l], lru_conv_b[l], lru_w_a[l], lru_b_a[l],
                          lru_w_x[l], lru_b_x[l], lru_lambda[l])
        o_d = conformer_conv_mixer(cu, conf_conv_w[l], conf_conv_b[l], conf_ln_g[l], conf_ln_b[l])
        mixed = jnp.concatenate([o_a, o_b, o_c, o_d], axis=-1)
        x = x + mixed @ w_out[l]
        h = rmsnorm(x, norm_ffn_g[l])
        x = x + (jax.nn.silu(h @ ffn_w_gate[l]) * (h @ ffn_w_up[l])) @ ffn_w_down[l]
    return rmsnorm(x, final_norm_g)
```

```python
import functools

import jax
import jax.numpy as jnp
from jax import lax
from jax.experimental import pallas as pl
from jax.experimental.pallas import tpu as pltpu

F32 = jnp.float32
BF16 = jnp.bfloat16

GROUP_WIDTH = 1024
GLA_HEADS = 4
GLA_DK = 128
GLA_DV = 256
GLA_GATE_RANK = 16
GLA_GATE_TAU = 16.0
GLA_DIAG = 16
SB_HEADS = 8
SB_HEAD_DIM = 128
LRU_BLOCKS = 8
LRU_BLOCK_DIM = 128
LRU_CONV = 4
LRU_C = 8.0
CONV_KERNEL = 31
LANES = 128
SUBLANES = 8

OFF_GQ, OFF_GK, OFF_GV, OFF_GG = 0, 512, 1024, 2048
OFF_SQ, OFF_SK, OFF_SV = 3072, 4096, 5120
OFF_LX, OFF_LG, OFF_CU = 6144, 7168, 8192
N_PROJ = 10240

VMEM_SLACK = 6 << 20


def _params(sem, vmem_bytes):
    return pltpu.CompilerParams(dimension_semantics=sem, vmem_limit_bytes=int(vmem_bytes))


def _rmsnorm_kernel(x_ref, g_ref, o_ref, *, eps):
    x = x_ref[...]
    ms = jnp.mean(x * x, axis=-1, keepdims=True)
    o_ref[...] = ((x * lax.rsqrt(ms + eps)) * g_ref[...]).astype(o_ref.dtype)


def rmsnorm(x, g, out_dtype, tm=256, eps=1e-6):
    m, d = x.shape
    return pl.pallas_call(
        functools.partial(_rmsnorm_kernel, eps=eps),
        grid=(m // tm,),
        in_specs=[pl.BlockSpec((tm, d), lambda i: (i, 0)),
                  pl.BlockSpec((1, d), lambda i: (0, 0))],
        out_specs=pl.BlockSpec((tm, d), lambda i: (i, 0)),
        out_shape=jax.ShapeDtypeStruct((m, d), out_dtype),
        compiler_params=_params(("parallel",), 4 * tm * d * 4 + VMEM_SLACK),
        name="rmsnorm",
    )(x, g.reshape(1, d))


def _matmul_kernel(*refs, n_a, has_res, nk):
    a_refs = refs[:n_a]
    w_ref = refs[n_a]
    res_ref = refs[n_a + 1] if has_res else None
    o_ref = refs[n_a + 1 + has_res]
    acc_ref = refs[n_a + 2 + has_res] if nk > 1 else None

    ka = a_refs[0].shape[1]
    part = None
    for g, a_ref in enumerate(a_refs):
        w = w_ref[...] if n_a == 1 else w_ref[g * ka:(g + 1) * ka, :]
        d = jnp.dot(a_ref[...], w, preferred_element_type=F32)
        part = d if part is None else part + d

    def finish(acc):
        if has_res:
            acc = acc + res_ref[...]
        o_ref[...] = acc.astype(o_ref.dtype)

    if nk == 1:
        finish(part)
    else:
        k = pl.program_id(2)

        @pl.when(k == 0)
        def _():
            acc_ref[...] = part

        @pl.when(k > 0)
        def _():
            acc_ref[...] += part

        @pl.when(k == nk - 1)
        def _():
            finish(acc_ref[...])


def matmul(a_list, w, res=None, *, out_dtype, tm, tn, tk=None):
    n_a = len(a_list)
    m, ka = a_list[0].shape
    k_total, n = w.shape
    assert ka * n_a == k_total
    tk = ka if tk is None else tk
    assert n_a == 1 or tk == ka
    nk = ka // tk
    assert ka % tk == 0 and m % tm == 0 and n % tn == 0
    in_specs = [pl.BlockSpec((tm, tk), lambda i, j, k: (i, k)) for _ in a_list]
    in_specs.append(pl.BlockSpec((tk * n_a, tn), lambda i, j, k: (k, j)))
    args = list(a_list) + [w]
    if res is not None:
        in_specs.append(pl.BlockSpec((tm, tn), lambda i, j, k: (i, j)))
        args.append(res)
    scratch = [pltpu.VMEM((tm, tn), F32)] if nk > 1 else []
    osz = jnp.dtype(out_dtype).itemsize
    vmem = (2 * (n_a * tm * tk * 2 + tk * n_a * tn * 2 + tm * tn * osz
                 + (tm * tn * 4 if res is not None else 0))
            + (tm * tn * 4 if nk > 1 else 0) + tm * tn * 4 + VMEM_SLACK)
    return pl.pallas_call(
        functools.partial(_matmul_kernel, n_a=n_a, has_res=res is not None, nk=nk),
        grid=(m // tm, n // tn, nk),
        in_specs=in_specs,
        out_specs=pl.BlockSpec((tm, tn), lambda i, j, k: (i, j)),
        out_shape=jax.ShapeDtypeStruct((m, n), out_dtype),
        scratch_shapes=scratch,
        compiler_params=_params(("parallel", "parallel", "arbitrary"), vmem),
        name="matmul",
    )(*args)


def _gate_up_kernel(a_ref, wg_ref, wu_ref, o_ref):
    a = a_ref[...]
    g = jnp.dot(a, wg_ref[...], preferred_element_type=F32)
    u = jnp.dot(a, wu_ref[...], preferred_element_type=F32)
    o_ref[...] = ((g * jax.nn.sigmoid(g)) * u).astype(o_ref.dtype)


def gate_up(a, wg, wu, *, tm, tn):
    m, k = a.shape
    n = wg.shape[1]
    vmem = 2 * (tm * k * 2 + 2 * k * tn * 2 + tm * tn * 2) + 3 * tm * tn * 4 + VMEM_SLACK
    return pl.pallas_call(
        _gate_up_kernel,
        grid=(m // tm, n // tn),
        in_specs=[pl.BlockSpec((tm, k), lambda i, j: (i, 0)),
                  pl.BlockSpec((k, tn), lambda i, j: (0, j)),
                  pl.BlockSpec((k, tn), lambda i, j: (0, j))],
        out_specs=pl.BlockSpec((tm, tn), lambda i, j: (i, j)),
        out_shape=jax.ShapeDtypeStruct((m, n), BF16),
        compiler_params=_params(("parallel", "parallel"), vmem),
        name="ffn_gate_up",
    )(a, wg, wu)


def _softplus_parts(z):
    l = jnp.log1p(jnp.exp(-jnp.abs(z)))
    return jnp.minimum(z, 0.0) - l, -jnp.maximum(z, 0.0) - l


def _split_bf16(x):
    hi = x.astype(BF16)
    lo = (x - hi.astype(F32)).astype(BF16)
    return hi, lo


def _dot_nt(a, b):
    return lax.dot_general(a, b, (((1,), (1,)), ((), ())), preferred_element_type=F32)


def _dot_tn(a, b):
    return lax.dot_general(a, b, (((0,), (0,)), ((), ())), preferred_element_type=F32)


def _iota2(shape, dim):
    return lax.broadcasted_iota(jnp.int32, shape, dim)


def _sb_kernel(q_ref, k_ref, v_ref, o_ref, *, tb, scale):
    i = pl.program_id(2)
    q = q_ref[...].astype(BF16)
    row = _iota2((tb, tb), 0)
    col = _iota2((tb, tb), 1)
    tri = jnp.where(row > col, 1.0, 0.0).astype(BF16)
    before = col < row

    def block(ks, carry, acc, diagonal):
        kb = k_ref[pl.ds(ks, tb), :].astype(BF16)
        vb = v_ref[pl.ds(ks, tb), :].astype(BF16)
        z = _dot_nt(q, kb) * scale
        log_beta, log_rest = _softplus_parts(z)
        if diagonal:
            log_rest = jnp.where(before, log_rest, 0.0)
        hi, lo = _split_bf16(log_rest)
        tail = (jnp.dot(hi, tri, preferred_element_type=F32)
                + jnp.dot(lo, tri, preferred_element_type=F32))
        w = jnp.exp(log_beta + tail + carry)
        if diagonal:
            w = jnp.where(before, w, 0.0)
        acc = acc + jnp.dot(w.astype(BF16), vb, preferred_element_type=F32)
        carry = carry + jnp.sum(log_rest, axis=-1, keepdims=True)
        return carry, acc

    carry0 = jnp.zeros((tb, 1), F32)
    acc0 = jnp.zeros((tb, o_ref.shape[1]), F32)
    carry, acc = block(pl.multiple_of(i * tb, tb), carry0, acc0, True)

    def body(jb, state):
        ks = pl.multiple_of((i - 1 - jb) * tb, tb)
        return block(ks, state[0], state[1], False)

    carry, acc = lax.fori_loop(0, i, body, (carry, acc))
    o_ref[...] = acc.astype(o_ref.dtype)


def stick_breaking(proj, bsz, t, *, tb=256):
    m = proj.shape[0]
    nq = t // tb
    cq, ck, cv = OFF_SQ // LANES, OFF_SK // LANES, OFF_SV // LANES
    vmem = 2 * (tb * LANES * 4 + 2 * t * LANES * 4 + tb * LANES * 2) + 12 * tb * tb * 4 + VMEM_SLACK
    return pl.pallas_call(
        functools.partial(_sb_kernel, tb=tb, scale=SB_HEAD_DIM ** -0.5),
        grid=(bsz, SB_HEADS, nq),
        in_specs=[pl.BlockSpec((tb, LANES), lambda b, h, i: (b * nq + i, cq + h)),
                  pl.BlockSpec((t, LANES), lambda b, h, i: (b, ck + h)),
                  pl.BlockSpec((t, LANES), lambda b, h, i: (b, cv + h))],
        out_specs=pl.BlockSpec((tb, LANES), lambda b, h, i: (b * nq + i, h)),
        out_shape=jax.ShapeDtypeStruct((m, GROUP_WIDTH), BF16),
        compiler_params=_params(("parallel", "parallel", "arbitrary"), vmem),
        name="stick_breaking",
    )(proj, proj, proj)


def _gla_kernel(gq_ref, gk_ref, gv_ref, gg_ref, glr_ref, w2_ref, bg_ref, ng_ref, o_ref, st_ref,
                *, tt, eps):
    @pl.when(pl.program_id(1) == 0)
    def _():
        st_ref[...] = jnp.zeros_like(st_ref)

    row = _iota2((tt, tt), 0)
    col = _iota2((tt, tt), 1)
    cmr = col - row

    gate = jnp.dot(glr_ref[...].astype(BF16), w2_ref[...], preferred_element_type=F32) + bg_ref[...]
    log_alpha = _softplus_parts(gate)[0] * (1.0 / GLA_GATE_TAU)
    la_hi, la_lo = _split_bf16(log_alpha)

    def rowsum(sel):
        s = sel.astype(BF16)
        return (jnp.dot(s, la_hi, preferred_element_type=F32)
                + jnp.dot(s, la_lo, preferred_element_type=F32))

    cum = rowsum(jnp.where(col <= row, 1.0, 0.0))
    last = cum[tt - 1:tt, :]

    levels = []
    s = tt // 2
    while s >= GLA_DIAG:
        start_r = (row & ~(2 * s - 1)) + s
        is_r = (row & s) != 0
        sel = jnp.where(is_r,
                        jnp.where((col >= start_r) & (col <= row), 1.0, 0.0),
                        jnp.where((col > row) & (col < start_r), 1.0, 0.0))
        same_pair = (row & ~(2 * s - 1)) == (col & ~(2 * s - 1))
        levels.append((s, jnp.exp(rowsum(sel)), same_pair))
        s //= 2

    q_all = gq_ref[...] * (GLA_DK ** -0.5)
    k_all = gk_ref[...]
    v_all = gv_ref[...]
    rowl = _iota2((tt, GLA_DK), 0)
    dv = GLA_DV
    for h in range(GLA_HEADS):
        ksl = slice(h * GLA_DK, (h + 1) * GLA_DK)
        vsl = slice(h * dv, (h + 1) * dv)
        q, k, c = q_all[:, ksl], k_all[:, ksl], cum[:, ksl]
        v_bf = v_all[:, vsl].astype(BF16)
        st = st_ref[h]

        o = _dot_nt((q * jnp.exp(c)).astype(BF16), st.astype(BF16))

        scores = jnp.zeros((tt, tt), F32)
        for s, decay, same_pair in levels:
            d = decay[:, ksl]
            is_r = (rowl & s) != 0
            qh = jnp.where(is_r, q * d, 0.0).astype(BF16)
            kh = jnp.where(is_r, 0.0, k * d).astype(BF16)
            sc = _dot_nt(qh, kh)
            scores = scores + (sc if 2 * s == tt else jnp.where(same_pair, sc, 0.0))

        for dd in range(GLA_DIAG):
            k_sh = k if dd == 0 else pltpu.roll(k, dd, 0)
            c_sh = c if dd == 0 else pltpu.roll(c, dd, 0)
            e = jnp.exp(jnp.minimum(c - c_sh, 0.0))
            sd = jnp.sum(q * k_sh * e, axis=-1, keepdims=True)
            ok = (cmr == -dd) & ((row & (GLA_DIAG - 1)) >= dd)
            scores = scores + jnp.where(ok, sd, 0.0)

        o = o + jnp.dot(scores.astype(BF16), v_bf, preferred_element_type=F32)

        lh = last[:, ksl]
        k_dec = (k * jnp.exp(lh - c)).astype(BF16)
        st_ref[h] = st * jnp.exp(lh) + _dot_tn(v_bf, k_dec)

        ms = jnp.mean(o * o, axis=-1, keepdims=True)
        y = (o * lax.rsqrt(ms + eps)) * ng_ref[...]
        g_out = gg_ref[:, vsl]
        o_ref[:, vsl] = (y * (g_out * jax.nn.sigmoid(g_out))).astype(o_ref.dtype)


def gla(proj, glr, w2, b_gate, norm_g, bsz, t, *, tt=128, eps=1e-6):
    m = proj.shape[0]
    nt = t // tt
    hk = GLA_HEADS * GLA_DK
    row = lambda b, i: b * nt + i
    vmem = (2 * (2 * tt * hk * 4 + 2 * tt * GROUP_WIDTH * 4 + tt * LANES * 4 + tt * GROUP_WIDTH * 2)
            + GLA_HEADS * GLA_DV * GLA_DK * 4 + 24 * tt * hk * 4 + VMEM_SLACK)
    return pl.pallas_call(
        functools.partial(_gla_kernel, tt=tt, eps=eps),
        grid=(bsz, nt),
        in_specs=[pl.BlockSpec((tt, hk), lambda b, i: (row(b, i), OFF_GQ // hk)),
                  pl.BlockSpec((tt, hk), lambda b, i: (row(b, i), OFF_GK // hk)),
                  pl.BlockSpec((tt, GROUP_WIDTH), lambda b, i: (row(b, i), OFF_GV // GROUP_WIDTH)),
                  pl.BlockSpec((tt, GROUP_WIDTH), lambda b, i: (row(b, i), OFF_GG // GROUP_WIDTH)),
                  pl.BlockSpec((tt, LANES), lambda b, i: (row(b, i), 0)),
                  pl.BlockSpec((LANES, hk), lambda b, i: (0, 0)),
                  pl.BlockSpec((1, hk), lambda b, i: (0, 0)),
                  pl.BlockSpec((1, GLA_DV), lambda b, i: (0, 0))],
        out_specs=pl.BlockSpec((tt, GROUP_WIDTH), lambda b, i: (row(b, i), 0)),
        out_shape=jax.ShapeDtypeStruct((m, GROUP_WIDTH), BF16),
        scratch_shapes=[pltpu.VMEM((GLA_HEADS, GLA_DV, GLA_DK), F32)],
        compiler_params=_params(("parallel", "arbitrary"), vmem),
        name="gla",
    )(proj, proj, proj, proj, glr, w2, b_gate.reshape(1, hk), norm_g.reshape(1, GLA_DV))


def _lru_kernel(lx_ref, lg_ref, cw_ref, cb_ref, wax_ref, ba_ref, bx_ref, lam_ref, o_ref,
                ext_ref, h_ref, *, tt):
    pad = SUBLANES

    @pl.when(pl.program_id(1) == 0)
    def _():
        ext_ref[0:pad, :] = jnp.zeros((pad, ext_ref.shape[1]), F32)
        h_ref[...] = jnp.zeros_like(h_ref)

    ext_ref[pad:pad + tt, :] = lx_ref[...]
    xc = cb_ref[...] + cw_ref[0:1, :] * ext_ref[pl.ds(pad - LRU_CONV + 1, tt), :]
    for kk in range(1, LRU_CONV):
        xc = xc + cw_ref[kk:kk + 1, :] * ext_ref[pl.ds(pad - LRU_CONV + 1 + kk, tt), :]
    ext_ref[0:pad, :] = ext_ref[tt:tt + pad, :]

    bd = LRU_BLOCK_DIM
    r_parts, i_parts = [], []
    for n in range(LRU_BLOCKS):
        ri = jnp.dot(xc[:, n * bd:(n + 1) * bd].astype(BF16), wax_ref[n], preferred_element_type=F32)
        r_parts.append(ri[:, :bd])
        i_parts.append(ri[:, bd:])
    r = jax.nn.sigmoid(jnp.concatenate(r_parts, axis=1) + ba_ref[...])
    ig = jax.nn.sigmoid(jnp.concatenate(i_parts, axis=1) + bx_ref[...])

    lam = lam_ref[...]
    softplus_neg_lam = jnp.maximum(-lam, 0.0) + jnp.log1p(jnp.exp(-jnp.abs(lam)))
    log_a = (-LRU_C) * r * softplus_neg_lam
    a = jnp.exp(log_a)
    u = jnp.sqrt(-jnp.tanh(log_a) * (a * a + 1.0)) * (ig * xc)

    rowi = _iota2(a.shape, 0)
    sft = 1
    while sft < tt:
        keep = rowi >= sft
        a_prev = jnp.where(keep, pltpu.roll(a, sft, 0), 1.0)
        u_prev = jnp.where(keep, pltpu.roll(u, sft, 0), 0.0)
        u = a * u_prev + u
        a = a * a_prev
        sft *= 2
    h = u + a * h_ref[0:1, :]
    h_ref[0:1, :] = h[tt - 1:tt, :]

    lg = lg_ref[...]
    gelu = 0.5 * lg * (1.0 + jnp.tanh(0.7978845608028654 * (lg + 0.044715 * (lg * lg * lg))))
    o_ref[...] = (h * gelu).astype(o_ref.dtype)


def rglru(proj, conv_w, conv_b, wax, b_a, b_x, lam, bsz, t, *, tt=256):
    m = proj.shape[0]
    nt = t // tt
    w = GROUP_WIDTH
    row = lambda b, i: b * nt + i
    vec = lambda: pl.BlockSpec((1, w), lambda b, i: (0, 0))
    vmem = 2 * (2 * tt * w * 4 + tt * w * 2) + 24 * tt * w * 4 + VMEM_SLACK
    return pl.pallas_call(
        functools.partial(_lru_kernel, tt=tt),
        grid=(bsz, nt),
        in_specs=[pl.BlockSpec((tt, w), lambda b, i: (row(b, i), OFF_LX // w)),
                  pl.BlockSpec((tt, w), lambda b, i: (row(b, i), OFF_LG // w)),
                  pl.BlockSpec((LRU_CONV, w), lambda b, i: (0, 0)),
                  vec(),
                  pl.BlockSpec((LRU_BLOCKS, LRU_BLOCK_DIM, 2 * LRU_BLOCK_DIM), lambda b, i: (0, 0, 0)),
                  vec(), vec(), vec()],
        out_specs=pl.BlockSpec((tt, w), lambda b, i: (row(b, i), 0)),
        out_shape=jax.ShapeDtypeStruct((m, w), BF16),
        scratch_shapes=[pltpu.VMEM((tt + SUBLANES, w), F32), pltpu.VMEM((SUBLANES, w), F32)],
        compiler_params=_params(("parallel", "arbitrary"), vmem),
        name="rglru",
    )(proj, proj, conv_w, conv_b.reshape(1, w), wax, b_a.reshape(1, w), b_x.reshape(1, w),
      lam.reshape(1, w))


CONF_PAD = 32


def _conformer_kernel(val_ref, gte_ref, cw_ref, cb_ref, g_ref, b_ref, o_ref, ext_ref, *, tt, eps):
    pad = CONF_PAD

    @pl.when(pl.program_id(1) == 0)
    def _():
        ext_ref[0:pad, :] = jnp.zeros((pad, ext_ref.shape[1]), F32)

    ext_ref[pad:pad + tt, :] = val_ref[...] * jax.nn.sigmoid(gte_ref[...])
    base = pad - CONV_KERNEL + 1
    y = cb_ref[...] + cw_ref[0:1, :] * ext_ref[pl.ds(base, tt), :]
    for kk in range(1, CONV_KERNEL):
        y = y + cw_ref[kk:kk + 1, :] * ext_ref[pl.ds(base + kk, tt), :]
    ext_ref[0:pad, :] = ext_ref[tt:tt + pad, :]

    mu = jnp.mean(y, axis=-1, keepdims=True)
    yc = y - mu
    var = jnp.mean(yc * yc, axis=-1, keepdims=True)
    z = (yc * lax.rsqrt(var + eps)) * g_ref[...] + b_ref[...]
    o_ref[...] = (z * jax.nn.sigmoid(z)).astype(o_ref.dtype)


def conformer(proj, conv_w, conv_b, ln_g, ln_b, bsz, t, *, tt=256, eps=1e-5):
    m = proj.shape[0]
    nt = t // tt
    w = GROUP_WIDTH
    row = lambda b, i: b * nt + i
    vec = lambda: pl.BlockSpec((1, w), lambda b, i: (0, 0))
    vmem = 2 * (2 * tt * w * 4 + tt * w * 2) + 8 * tt * w * 4 + VMEM_SLACK
    return pl.pallas_call(
        functools.partial(_conformer_kernel, tt=tt, eps=eps),
        grid=(bsz, nt),
        in_specs=[pl.BlockSpec((tt, w), lambda b, i: (row(b, i), OFF_CU // w)),
                  pl.BlockSpec((tt, w), lambda b, i: (row(b, i), OFF_CU // w + 1)),
                  pl.BlockSpec((CONV_KERNEL, w), lambda b, i: (0, 0)),
                  vec(), vec(), vec()],
        out_specs=pl.BlockSpec((tt, w), lambda b, i: (row(b, i), 0)),
        out_shape=jax.ShapeDtypeStruct((m, w), BF16),
        scratch_shapes=[pltpu.VMEM((tt + CONF_PAD, w), F32)],
        compiler_params=_params(("parallel", "arbitrary"), vmem),
        name="conformer",
    )(proj, proj, conv_w, conv_b.reshape(1, w), ln_g.reshape(1, w), ln_b.reshape(1, w))


def _reorder_w_in(w):
    lo = OFF_SQ
    main = jnp.concatenate([w[:, :lo], w[:, lo + GLA_GATE_RANK:]], axis=1).astype(BF16)
    rank = jnp.pad(w[:, lo:lo + GLA_GATE_RANK], ((0, 0), (0, LANES - GLA_GATE_RANK))).astype(BF16)
    return main, rank


def kernel(x, norm_mix_g, w_in, gla_w_gate2, gla_b_gate, gla_norm_g, lru_conv_w, lru_conv_b, lru_w_a, lru_b_a, lru_w_x, lru_b_x, lru_lambda, conf_conv_w, conf_conv_b, conf_ln_g, conf_ln_b, w_out, norm_ffn_g, ffn_w_gate, ffn_w_up, ffn_w_down, final_norm_g):
    bsz, t, d = x.shape
    depth = w_in.shape[0]
    xf = x.reshape(bsz * t, d)
    for l in range(depth):
        w_main, w_rank = _reorder_w_in(w_in[l])
        w2 = jnp.pad(gla_w_gate2[l], ((0, LANES - GLA_GATE_RANK), (0, 0))).astype(BF16)
        wax = jnp.concatenate([lru_w_a[l], lru_w_x[l]], axis=-1).astype(BF16)

        h = rmsnorm(xf, norm_mix_g[l], BF16)
        proj = matmul([h], w_main, out_dtype=F32, tm=1024, tn=1024)
        glr = matmul([h], w_rank, out_dtype=F32, tm=1024, tn=LANES)
        o_a = gla(proj, glr, w2, gla_b_gate[l], gla_norm_g[l], bsz, t)
        o_b = stick_breaking(proj, bsz, t)
        o_c = rglru(proj, lru_conv_w[l], lru_conv_b[l], wax, lru_b_a[l], lru_b_x[l], lru_lambda[l], bsz, t)
        o_d = conformer(proj, conf_conv_w[l], conf_conv_b[l], conf_ln_g[l], conf_ln_b[l], bsz, t)
        xf = matmul([o_a, o_b, o_c, o_d], w_out[l].astype(BF16), xf, out_dtype=F32, tm=1024, tn=512)

        h = rmsnorm(xf, norm_ffn_g[l], BF16)
        act = gate_up(h, ffn_w_gate[l].astype(BF16), ffn_w_up[l].astype(BF16), tm=1024, tn=256)
        hidden = act.shape[1]
        xf = matmul([act], ffn_w_down[l].astype(BF16), xf, out_dtype=F32, tm=1024, tn=512,
                    tk=hidden // 2)
    return rmsnorm(xf, final_norm_g, F32).reshape(bsz, t, d)
```

```python
import functools

import jax
import jax.numpy as jnp
from jax import lax
from jax.experimental import pallas as pl
from jax.experimental.pallas import tpu as pltpu

F32 = jnp.float32
BF16 = jnp.bfloat16

GROUP_WIDTH = 1024
GLA_HEADS = 4
GLA_DK = 128
GLA_DV = 256
GLA_GATE_RANK = 16
GLA_GATE_TAU = 16.0
GLA_DIAG = 16
SB_HEADS = 8
SB_HEAD_DIM = 128
LRU_BLOCKS = 8
LRU_BLOCK_DIM = 128
LRU_CONV = 4
LRU_C = 8.0
CONV_KERNEL = 31
LANES = 128
SUBLANES = 8

OFF_GQ, OFF_GK, OFF_GV, OFF_GG = 0, 512, 1024, 2048
N_PROJ_GLA = 3072
OFF_SQ, OFF_SK, OFF_SV = 0, 1024, 2048
OFF_LX, OFF_LG, OFF_CU = 3072, 4096, 5120
N_PROJ_REST = 7168
SB_DEAD_LOG = -105.0

VMEM_SLACK = 6 << 20


def _params(sem, vmem_bytes):
    return pltpu.CompilerParams(dimension_semantics=sem, vmem_limit_bytes=int(vmem_bytes))


def _rmsnorm_kernel(x_ref, g_ref, o_ref, *, eps):
    x = x_ref[...]
    ms = jnp.mean(x * x, axis=-1, keepdims=True)
    o_ref[...] = ((x * lax.rsqrt(ms + eps)) * g_ref[...]).astype(o_ref.dtype)


def rmsnorm(x, g, out_dtype, tm=256, eps=1e-6):
    m, d = x.shape
    return pl.pallas_call(
        functools.partial(_rmsnorm_kernel, eps=eps),
        grid=(m // tm,),
        in_specs=[pl.BlockSpec((tm, d), lambda i: (i, 0)),
                  pl.BlockSpec((1, d), lambda i: (0, 0))],
        out_specs=pl.BlockSpec((tm, d), lambda i: (i, 0)),
        out_shape=jax.ShapeDtypeStruct((m, d), out_dtype),
        compiler_params=_params(("parallel",), 4 * tm * d * 4 + VMEM_SLACK),
        name="rmsnorm",
    )(x, g.reshape(1, d))


def _matmul_kernel(*refs, n_a, has_res, nk):
    a_refs = refs[:n_a]
    w_ref = refs[n_a]
    res_ref = refs[n_a + 1] if has_res else None
    o_ref = refs[n_a + 1 + has_res]
    acc_ref = refs[n_a + 2 + has_res] if nk > 1 else None

    ka = a_refs[0].shape[1]
    part = None
    for g, a_ref in enumerate(a_refs):
        w = w_ref[...] if n_a == 1 else w_ref[g * ka:(g + 1) * ka, :]
        d = jnp.dot(a_ref[...], w, preferred_element_type=F32)
        part = d if part is None else part + d

    def finish(acc):
        if has_res:
            acc = acc + res_ref[...]
        o_ref[...] = acc.astype(o_ref.dtype)

    if nk == 1:
        finish(part)
    else:
        k = pl.program_id(2)

        @pl.when(k == 0)
        def _():
            acc_ref[...] = part

        @pl.when(k > 0)
        def _():
            acc_ref[...] += part

        @pl.when(k == nk - 1)
        def _():
            finish(acc_ref[...])


def matmul(a_list, w, res=None, *, out_dtype, tm, tn, tk=None):
    n_a = len(a_list)
    m, ka = a_list[0].shape
    k_total, n = w.shape
    assert ka * n_a == k_total
    tk = ka if tk is None else tk
    assert n_a == 1 or tk == ka
    nk = ka // tk
    assert ka % tk == 0 and m % tm == 0 and n % tn == 0
    in_specs = [pl.BlockSpec((tm, tk), lambda i, j, k: (i, k)) for _ in a_list]
    in_specs.append(pl.BlockSpec((tk * n_a, tn), lambda i, j, k: (k, j)))
    args = list(a_list) + [w]
    if res is not None:
        in_specs.append(pl.BlockSpec((tm, tn), lambda i, j, k: (i, j)))
        args.append(res)
    scratch = [pltpu.VMEM((tm, tn), F32)] if nk > 1 else []
    osz = jnp.dtype(out_dtype).itemsize
    vmem = (2 * (n_a * tm * tk * 2 + tk * n_a * tn * 2 + tm * tn * osz
                 + (tm * tn * 4 if res is not None else 0))
            + (tm * tn * 4 if nk > 1 else 0) + tm * tn * 4 + VMEM_SLACK)
    return pl.pallas_call(
        functools.partial(_matmul_kernel, n_a=n_a, has_res=res is not None, nk=nk),
        grid=(m // tm, n // tn, nk),
        in_specs=in_specs,
        out_specs=pl.BlockSpec((tm, tn), lambda i, j, k: (i, j)),
        out_shape=jax.ShapeDtypeStruct((m, n), out_dtype),
        scratch_shapes=scratch,
        compiler_params=_params(("parallel", "parallel", "arbitrary"), vmem),
        name="matmul",
    )(*args)


CAST_ROWS = 512


def _ws_matmul_kernel(*refs, n_a, n_w, shift, has_res, has_side, swiglu):
    a_refs = refs[:n_a]
    w_refs = refs[n_a:n_a + n_w]
    pos = n_a + n_w
    x_refs = refs[pos:pos + n_w] if shift else (None,) * n_w
    pos += n_w if shift else 0
    res_ref = refs[pos] if has_res else None
    pos += has_res
    side_ref = refs[pos] if has_side else None
    pos += has_side
    o_ref = refs[pos]
    pos += 1
    if has_side:
        refs[pos][...] = side_ref[...].astype(BF16)
        pos += 1
    wb_refs = refs[pos:pos + n_w]

    @pl.when(pl.program_id(1) == 0)
    def _():
        for w_ref, x_ref, wb_ref in zip(w_refs, x_refs, wb_refs):
            k, tn = w_ref.shape
            for r in range(0, k, CAST_ROWS):
                w = w_ref[r:r + CAST_ROWS, :]
                if shift:
                    cat = jnp.concatenate([w, x_ref[r:r + CAST_ROWS, :]], axis=1)
                    w = pltpu.roll(cat, cat.shape[1] - shift, 1)[:, :tn]
                wb_ref[r:r + CAST_ROWS, :] = w.astype(BF16)

    ka = a_refs[0].shape[1]
    outs = []
    for wb_ref in wb_refs:
        acc = None
        for g, a_ref in enumerate(a_refs):
            d = jnp.dot(a_ref[...], wb_ref[g * ka:(g + 1) * ka, :], preferred_element_type=F32)
            acc = d if acc is None else acc + d
        outs.append(acc)
    if swiglu:
        gate, up = outs
        out = (gate * jax.nn.sigmoid(gate)) * up
    else:
        out = outs[0]
    if has_res:
        out = out + res_ref[...]
    o_ref[...] = out.astype(o_ref.dtype)


def ws_matmul(a_list, w_list, layer, res=None, *, out_dtype, tm, tn, col0=0, ncols=None, shift=0,
              swiglu=False, side=None, name="ws_matmul"):
    n_a, n_w = len(a_list), len(w_list)
    m, ka = a_list[0].shape
    _, k_total, n_total = w_list[0].shape
    ncols = n_total if ncols is None else ncols
    assert ka * n_a == k_total and m % tm == 0 and ncols % tn == 0 and col0 % tn == 0
    assert k_total % CAST_ROWS == 0 and n_w == (2 if swiglu else 1)
    cb = col0 // tn
    in_specs = [pl.BlockSpec((tm, ka), lambda j, i: (i, 0)) for _ in a_list]
    in_specs += [pl.BlockSpec((None, k_total, tn), lambda j, i: (layer, 0, cb + j)) for _ in w_list]
    args = list(a_list) + list(w_list)
    if shift:
        per = tn // LANES
        in_specs += [pl.BlockSpec((None, k_total, LANES), lambda j, i: (layer, 0, (cb + j + 1) * per))
                     for _ in w_list]
        args += list(w_list)
    if res is not None:
        in_specs.append(pl.BlockSpec((tm, tn), lambda j, i: (i, j)))
        args.append(res)
    nm = m // tm
    out_specs = [pl.BlockSpec((tm, tn), lambda j, i: (i, j))]
    out_shape = [jax.ShapeDtypeStruct((m, ncols), out_dtype)]
    side_bytes = 0
    if side is not None:
        _, srows, scols = side.shape
        slab = srows // ((ncols // tn) * nm)
        assert slab * (ncols // tn) * nm == srows and slab % (2 * SUBLANES) == 0
        in_specs.append(pl.BlockSpec((None, slab, scols), lambda j, i: (layer, j * nm + i, 0)))
        args.append(side)
        out_specs.append(pl.BlockSpec((slab, scols), lambda j, i: (j * nm + i, 0)))
        out_shape.append(jax.ShapeDtypeStruct((srows, scols), BF16))
        side_bytes = slab * scols * 6
    osz = jnp.dtype(out_dtype).itemsize
    vmem = (2 * (n_a * tm * ka * 2 + n_w * k_total * (tn + (LANES if shift else 0)) * 4
                 + tm * tn * osz + (tm * tn * 4 if res is not None else 0) + side_bytes)
            + n_w * k_total * tn * 2 + (n_w + 1) * tm * tn * 4 + VMEM_SLACK)
    outs = pl.pallas_call(
        functools.partial(_ws_matmul_kernel, n_a=n_a, n_w=n_w, shift=shift,
                          has_res=res is not None, has_side=side is not None, swiglu=swiglu),
        grid=(ncols // tn, nm),
        in_specs=in_specs,
        out_specs=out_specs,
        out_shape=out_shape,
        scratch_shapes=[pltpu.VMEM((k_total, tn), BF16) for _ in w_list],
        compiler_params=_params(("parallel", "arbitrary"), vmem),
        name=name,
    )(*args)
    return outs[0] if side is None else tuple(outs)


def _softplus_parts(z):
    l = jnp.log(1.0 + jnp.exp(-jnp.abs(z)))
    return jnp.minimum(z, 0.0) - l, -jnp.maximum(z, 0.0) - l


def _split_bf16(x):
    hi = x.astype(BF16)
    lo = (x - hi.astype(F32)).astype(BF16)
    return hi, lo


def _dot_nt(a, b):
    return lax.dot_general(a, b, (((1,), (1,)), ((), ())), preferred_element_type=F32)


def _dot_tn(a, b):
    return lax.dot_general(a, b, (((0,), (0,)), ((), ())), preferred_element_type=F32)


def _iota2(shape, dim):
    return lax.broadcasted_iota(jnp.int32, shape, dim)


def _sb_kernel(q_ref, k_ref, v_ref, o_ref, *, tb, scale):
    i = pl.program_id(2)
    hd = SB_HEAD_DIM
    nh = q_ref.shape[1] // hd
    row = _iota2((tb, tb), 0)
    col = _iota2((tb, tb), 1)
    tri = jnp.where(row > col, 1.0, 0.0).astype(BF16)
    before = col < row
    qs = [q_ref[:, h * hd:(h + 1) * hd].astype(BF16) for h in range(nh)]

    def block(h, ks, carry, acc, diagonal):
        kb = k_ref[pl.ds(ks, tb), h * hd:(h + 1) * hd].astype(BF16)
        vb = v_ref[pl.ds(ks, tb), h * hd:(h + 1) * hd].astype(BF16)
        z = _dot_nt(qs[h], kb) * scale
        log_beta, log_rest = _softplus_parts(z)
        if diagonal:
            log_rest = jnp.where(before, log_rest, 0.0)
        hi, lo = _split_bf16(log_rest)
        tail = (jnp.dot(hi, tri, preferred_element_type=F32)
                + jnp.dot(lo, tri, preferred_element_type=F32))
        w = jnp.exp(log_beta + tail + carry)
        if diagonal:
            w = jnp.where(before, w, 0.0)
        acc = acc + jnp.dot(w.astype(BF16), vb, preferred_element_type=F32)
        carry = carry + jnp.sum(log_rest, axis=-1, keepdims=True)
        return carry, acc

    def alive(carries):
        top = carries[0]
        for c in carries[1:]:
            top = jnp.maximum(top, c)
        return (jnp.max(top) >= SB_DEAD_LOG).astype(jnp.int32)

    k0 = pl.multiple_of(i * tb, tb)
    state = [block(h, k0, jnp.zeros((tb, 1), F32), jnp.zeros((tb, hd), F32), True)
             for h in range(nh)]
    carries = tuple(s[0] for s in state)
    accs = tuple(s[1] for s in state)

    def cond(st):
        return (st[0] < i) & (st[1] > 0)

    def body(st):
        jb, _, carries, accs = st
        ks = pl.multiple_of((i - 1 - jb) * tb, tb)
        new = [block(h, ks, carries[h], accs[h], False) for h in range(nh)]
        carries = tuple(s[0] for s in new)
        return jb + 1, alive(carries), carries, tuple(s[1] for s in new)

    _, _, _, accs = lax.while_loop(cond, body, (jnp.int32(0), alive(carries), carries, accs))
    for h in range(nh):
        o_ref[:, h * hd:(h + 1) * hd] = accs[h].astype(o_ref.dtype)


def stick_breaking(proj, bsz, t, *, tb=256, heads_per_step=2):
    m = proj.shape[0]
    nq = t // tb
    wd = heads_per_step * SB_HEAD_DIM
    cq, ck, cv = OFF_SQ // wd, OFF_SK // wd, OFF_SV // wd
    vmem = (2 * (tb * wd * 4 + 2 * t * wd * 4 + tb * wd * 2)
            + heads_per_step * 12 * tb * tb * 4 + VMEM_SLACK)
    return pl.pallas_call(
        functools.partial(_sb_kernel, tb=tb, scale=SB_HEAD_DIM ** -0.5),
        grid=(bsz, SB_HEADS // heads_per_step, nq),
        in_specs=[pl.BlockSpec((tb, wd), lambda b, h, i: (b * nq + i, cq + h)),
                  pl.BlockSpec((t, wd), lambda b, h, i: (b, ck + h)),
                  pl.BlockSpec((t, wd), lambda b, h, i: (b, cv + h))],
        out_specs=pl.BlockSpec((tb, wd), lambda b, h, i: (b * nq + i, h)),
        out_shape=jax.ShapeDtypeStruct((m, GROUP_WIDTH), BF16),
        compiler_params=_params(("parallel", "parallel", "arbitrary"), vmem),
        name="stick_breaking",
    )(proj, proj, proj)


def _gla_kernel(gq_ref, gk_ref, gv_ref, gg_ref, glr_ref, w2_ref, bg_ref, ng_ref, o_ref, st_ref,
                *, tt, eps):
    @pl.when(pl.program_id(1) == 0)
    def _():
        st_ref[...] = jnp.zeros_like(st_ref)

    row = _iota2((tt, tt), 0)
    col = _iota2((tt, tt), 1)
    cmr = col - row

    gate = jnp.dot(glr_ref[...].astype(BF16), w2_ref[...], preferred_element_type=F32) + bg_ref[...]
    log_alpha = _softplus_parts(gate)[0] * (1.0 / GLA_GATE_TAU)
    la_hi, la_lo = _split_bf16(log_alpha)

    def rowsum(sel):
        s = sel.astype(BF16)
        return (jnp.dot(s, la_hi, preferred_element_type=F32)
                + jnp.dot(s, la_lo, preferred_element_type=F32))

    cum = rowsum(jnp.where(col <= row, 1.0, 0.0))
    last = cum[tt - 1:tt, :]

    levels = []
    s = tt // 2
    while s >= GLA_DIAG:
        start_r = (row & ~(2 * s - 1)) + s
        is_r = (row & s) != 0
        sel = jnp.where(is_r,
                        jnp.where((col >= start_r) & (col <= row), 1.0, 0.0),
                        jnp.where((col > row) & (col < start_r), 1.0, 0.0))
        same_pair = (row & ~(2 * s - 1)) == (col & ~(2 * s - 1))
        levels.append((s, jnp.exp(rowsum(sel)), same_pair))
        s //= 2

    q_all = gq_ref[...] * (GLA_DK ** -0.5)
    k_all = gk_ref[...]
    v_all = gv_ref[...]
    rowl = _iota2((tt, GLA_DK), 0)
    dv = GLA_DV
    for h in range(GLA_HEADS):
        ksl = slice(h * GLA_DK, (h + 1) * GLA_DK)
        vsl = slice(h * dv, (h + 1) * dv)
        q, k, c = q_all[:, ksl], k_all[:, ksl], cum[:, ksl]
        v_bf = v_all[:, vsl].astype(BF16)
        st = st_ref[h]

        o = _dot_nt((q * jnp.exp(c)).astype(BF16), st.astype(BF16))

        scores = jnp.zeros((tt, tt), F32)
        for s, decay, same_pair in levels:
            d = decay[:, ksl]
            is_r = (rowl & s) != 0
            qh = jnp.where(is_r, q * d, 0.0).astype(BF16)
            kh = jnp.where(is_r, 0.0, k * d).astype(BF16)
            sc = _dot_nt(qh, kh)
            scores = scores + (sc if 2 * s == tt else jnp.where(same_pair, sc, 0.0))

        for dd in range(GLA_DIAG):
            k_sh = k if dd == 0 else pltpu.roll(k, dd, 0)
            c_sh = c if dd == 0 else pltpu.roll(c, dd, 0)
            e = jnp.exp(jnp.minimum(c - c_sh, 0.0))
            sd = jnp.sum(q * k_sh * e, axis=-1, keepdims=True)
            ok = (cmr == -dd) & ((row & (GLA_DIAG - 1)) >= dd)
            scores = scores + jnp.where(ok, sd, 0.0)

        o = o + jnp.dot(scores.astype(BF16), v_bf, preferred_element_type=F32)

        lh = last[:, ksl]
        k_dec = (k * jnp.exp(lh - c)).astype(BF16)
        st_ref[h] = st * jnp.exp(lh) + _dot_tn(v_bf, k_dec)

        ms = jnp.mean(o * o, axis=-1, keepdims=True)
        y = (o * lax.rsqrt(ms + eps)) * ng_ref[...]
        g_out = gg_ref[:, vsl]
        o_ref[:, vsl] = (y * (g_out * jax.nn.sigmoid(g_out))).astype(o_ref.dtype)


def gla(proj, glr, w2, b_gate, norm_g, bsz, t, *, tt=128, eps=1e-6):
    m = proj.shape[0]
    nt = t // tt
    hk = GLA_HEADS * GLA_DK
    row = lambda b, i: b * nt + i
    vmem = (2 * (2 * tt * hk * 4 + 2 * tt * GROUP_WIDTH * 4 + tt * LANES * 4 + tt * GROUP_WIDTH * 2)
            + GLA_HEADS * GLA_DV * GLA_DK * 4 + 24 * tt * hk * 4 + VMEM_SLACK)
    return pl.pallas_call(
        functools.partial(_gla_kernel, tt=tt, eps=eps),
        grid=(bsz, nt),
        in_specs=[pl.BlockSpec((tt, hk), lambda b, i: (row(b, i), OFF_GQ // hk)),
                  pl.BlockSpec((tt, hk), lambda b, i: (row(b, i), OFF_GK // hk)),
                  pl.BlockSpec((tt, GROUP_WIDTH), lambda b, i: (row(b, i), OFF_GV // GROUP_WIDTH)),
                  pl.BlockSpec((tt, GROUP_WIDTH), lambda b, i: (row(b, i), OFF_GG // GROUP_WIDTH)),
                  pl.BlockSpec((tt, LANES), lambda b, i: (row(b, i), 0)),
                  pl.BlockSpec((LANES, hk), lambda b, i: (0, 0)),
                  pl.BlockSpec((1, hk), lambda b, i: (0, 0)),
                  pl.BlockSpec((1, GLA_DV), lambda b, i: (0, 0))],
        out_specs=pl.BlockSpec((tt, GROUP_WIDTH), lambda b, i: (row(b, i), 0)),
        out_shape=jax.ShapeDtypeStruct((m, GROUP_WIDTH), BF16),
        scratch_shapes=[pltpu.VMEM((GLA_HEADS, GLA_DV, GLA_DK), F32)],
        compiler_params=_params(("parallel", "arbitrary"), vmem),
        name="gla",
    )(proj, proj, proj, proj, glr, w2, b_gate.reshape(1, hk), norm_g.reshape(1, GLA_DV))


def _lru_kernel(lx_ref, lg_ref, cw_ref, cb_ref, wax_ref, ba_ref, bx_ref, lam_ref, o_ref,
                ext_ref, h_ref, *, tt):
    pad = SUBLANES

    @pl.when(pl.program_id(1) == 0)
    def _():
        ext_ref[0:pad, :] = jnp.zeros((pad, ext_ref.shape[1]), F32)
        h_ref[...] = jnp.zeros_like(h_ref)

    ext_ref[pad:pad + tt, :] = lx_ref[...]
    xc = cb_ref[...] + cw_ref[0:1, :] * ext_ref[pl.ds(pad - LRU_CONV + 1, tt), :]
    for kk in range(1, LRU_CONV):
        xc = xc + cw_ref[kk:kk + 1, :] * ext_ref[pl.ds(pad - LRU_CONV + 1 + kk, tt), :]
    ext_ref[0:pad, :] = ext_ref[tt:tt + pad, :]

    bd = LRU_BLOCK_DIM
    r_parts, i_parts = [], []
    for n in range(LRU_BLOCKS):
        ri = jnp.dot(xc[:, n * bd:(n + 1) * bd].astype(BF16), wax_ref[n], preferred_element_type=F32)
        r_parts.append(ri[:, :bd])
        i_parts.append(ri[:, bd:])
    r = jax.nn.sigmoid(jnp.concatenate(r_parts, axis=1) + ba_ref[...])
    ig = jax.nn.sigmoid(jnp.concatenate(i_parts, axis=1) + bx_ref[...])

    lam = lam_ref[...]
    softplus_neg_lam = jnp.maximum(-lam, 0.0) + jnp.log1p(jnp.exp(-jnp.abs(lam)))
    log_a = (-LRU_C) * r * softplus_neg_lam
    a = jnp.exp(log_a)
    u = jnp.sqrt(-jnp.tanh(log_a) * (a * a + 1.0)) * (ig * xc)

    rowi = _iota2(a.shape, 0)
    sft = 1
    while sft < tt:
        keep = rowi >= sft
        a_prev = jnp.where(keep, pltpu.roll(a, sft, 0), 1.0)
        u_prev = jnp.where(keep, pltpu.roll(u, sft, 0), 0.0)
        u = a * u_prev + u
        a = a * a_prev
        sft *= 2
    h = u + a * h_ref[0:1, :]
    h_ref[0:1, :] = h[tt - 1:tt, :]

    lg = lg_ref[...]
    gelu = 0.5 * lg * (1.0 + jnp.tanh(0.7978845608028654 * (lg + 0.044715 * (lg * lg * lg))))
    o_ref[...] = (h * gelu).astype(o_ref.dtype)


def rglru(proj, conv_w, conv_b, wax, b_a, b_x, lam, bsz, t, *, tt=256):
    m = proj.shape[0]
    nt = t // tt
    w = GROUP_WIDTH
    row = lambda b, i: b * nt + i
    vec = lambda: pl.BlockSpec((1, w), lambda b, i: (0, 0))
    vmem = 2 * (2 * tt * w * 4 + tt * w * 2) + 24 * tt * w * 4 + VMEM_SLACK
    return pl.pallas_call(
        functools.partial(_lru_kernel, tt=tt),
        grid=(bsz, nt),
        in_specs=[pl.BlockSpec((tt, w), lambda b, i: (row(b, i), OFF_LX // w)),
                  pl.BlockSpec((tt, w), lambda b, i: (row(b, i), OFF_LG // w)),
                  pl.BlockSpec((LRU_CONV, w), lambda b, i: (0, 0)),
                  vec(),
                  pl.BlockSpec((LRU_BLOCKS, LRU_BLOCK_DIM, 2 * LRU_BLOCK_DIM), lambda b, i: (0, 0, 0)),
                  vec(), vec(), vec()],
        out_specs=pl.BlockSpec((tt, w), lambda b, i: (row(b, i), 0)),
        out_shape=jax.ShapeDtypeStruct((m, w), BF16),
        scratch_shapes=[pltpu.VMEM((tt + SUBLANES, w), F32), pltpu.VMEM((SUBLANES, w), F32)],
        compiler_params=_params(("parallel", "arbitrary"), vmem),
        name="rglru",
    )(proj, proj, conv_w, conv_b.reshape(1, w), wax, b_a.reshape(1, w), b_x.reshape(1, w),
      lam.reshape(1, w))


CONF_PAD = 32


def _conformer_kernel(val_ref, gte_ref, cw_ref, cb_ref, g_ref, b_ref, o_ref, ext_ref, *, tt, eps):
    pad = CONF_PAD

    @pl.when(pl.program_id(1) == 0)
    def _():
        ext_ref[0:pad, :] = jnp.zeros((pad, ext_ref.shape[1]), F32)

    ext_ref[pad:pad + tt, :] = val_ref[...] * jax.nn.sigmoid(gte_ref[...])
    base = pad - CONV_KERNEL + 1
    y = cb_ref[...] + cw_ref[0:1, :] * ext_ref[pl.ds(base, tt), :]
    for kk in range(1, CONV_KERNEL):
        y = y + cw_ref[kk:kk + 1, :] * ext_ref[pl.ds(base + kk, tt), :]
    ext_ref[0:pad, :] = ext_ref[tt:tt + pad, :]

    mu = jnp.mean(y, axis=-1, keepdims=True)
    yc = y - mu
    var = jnp.mean(yc * yc, axis=-1, keepdims=True)
    z = (yc * lax.rsqrt(var + eps)) * g_ref[...] + b_ref[...]
    o_ref[...] = (z * jax.nn.sigmoid(z)).astype(o_ref.dtype)


def conformer(proj, conv_w, conv_b, ln_g, ln_b, bsz, t, *, tt=256, eps=1e-5):
    m = proj.shape[0]
    nt = t // tt
    w = GROUP_WIDTH
    row = lambda b, i: b * nt + i
    vec = lambda: pl.BlockSpec((1, w), lambda b, i: (0, 0))
    vmem = 2 * (2 * tt * w * 4 + tt * w * 2) + 8 * tt * w * 4 + VMEM_SLACK
    return pl.pallas_call(
        functools.partial(_conformer_kernel, tt=tt, eps=eps),
        grid=(bsz, nt),
        in_specs=[pl.BlockSpec((tt, w), lambda b, i: (row(b, i), OFF_CU // w)),
                  pl.BlockSpec((tt, w), lambda b, i: (row(b, i), OFF_CU // w + 1)),
                  pl.BlockSpec((CONV_KERNEL, w), lambda b, i: (0, 0)),
                  vec(), vec(), vec()],
        out_specs=pl.BlockSpec((tt, w), lambda b, i: (row(b, i), 0)),
        out_shape=jax.ShapeDtypeStruct((m, w), BF16),
        scratch_shapes=[pltpu.VMEM((tt + CONF_PAD, w), F32)],
        compiler_params=_params(("parallel", "arbitrary"), vmem),
        name="conformer",
    )(proj, proj, conv_w, conv_b.reshape(1, w), ln_g.reshape(1, w), ln_b.reshape(1, w))


def kernel(x, norm_mix_g, w_in, gla_w_gate2, gla_b_gate, gla_norm_g, lru_conv_w, lru_conv_b, lru_w_a, lru_b_a, lru_w_x, lru_b_x, lru_lambda, conf_conv_w, conf_conv_b, conf_ln_g, conf_ln_b, w_out, norm_ffn_g, ffn_w_gate, ffn_w_up, ffn_w_down, final_norm_g):
    bsz, t, d = x.shape
    depth = w_in.shape[0]
    xf = x.reshape(bsz * t, d)
    for l in range(depth):
        rank0 = N_PROJ_GLA
        w_rank = jnp.pad(w_in[l, :, rank0:rank0 + GLA_GATE_RANK],
                         ((0, 0), (0, LANES - GLA_GATE_RANK))).astype(BF16)
        w2 = jnp.pad(gla_w_gate2[l], ((0, LANES - GLA_GATE_RANK), (0, 0))).astype(BF16)
        wax = jnp.concatenate([lru_w_a[l], lru_w_x[l]], axis=-1).astype(BF16)

        h = rmsnorm(xf, norm_mix_g[l], BF16)
        proj_gla = ws_matmul([h], [w_in], l, out_dtype=F32, tm=1024, tn=512, ncols=N_PROJ_GLA,
                             name="in_proj_gla")
        proj = ws_matmul([h], [w_in], l, out_dtype=F32, tm=1024, tn=512, col0=rank0,
                         ncols=N_PROJ_REST, shift=GLA_GATE_RANK, name="in_proj_rest")
        glr = matmul([h], w_rank, out_dtype=F32, tm=1024, tn=LANES)
        o_a = gla(proj_gla, glr, w2, gla_b_gate[l], gla_norm_g[l], bsz, t)
        o_b = stick_breaking(proj, bsz, t)
        o_c = rglru(proj, lru_conv_w[l], lru_conv_b[l], wax, lru_b_a[l], lru_b_x[l], lru_lambda[l], bsz, t)
        o_d = conformer(proj, conf_conv_w[l], conf_conv_b[l], conf_ln_g[l], conf_ln_b[l], bsz, t)
        xf = ws_matmul([o_a, o_b, o_c, o_d], [w_out], l, xf, out_dtype=F32, tm=1024, tn=512,
                       name="out_proj")

        h = rmsnorm(xf, norm_ffn_g[l], BF16)
        act, w_down = ws_matmul([h], [ffn_w_gate, ffn_w_up], l, out_dtype=BF16, tm=1024, tn=256,
                                swiglu=True, side=ffn_w_down, name="ffn_gate_up")
        hidden = act.shape[1]
        xf = matmul([act], w_down, xf, out_dtype=F32, tm=1024, tn=512, tk=hidden // 2)
    return rmsnorm(xf, final_norm_g, F32).reshape(bsz, t, d)
```

```python
import functools

import jax
import jax.numpy as jnp
from jax import lax
from jax.experimental import pallas as pl
from jax.experimental.pallas import tpu as pltpu

F32 = jnp.float32
BF16 = jnp.bfloat16

GROUP_WIDTH = 1024
GLA_HEADS = 4
GLA_DK = 128
GLA_DV = 256
GLA_GATE_RANK = 16
GLA_GATE_TAU = 16.0
GLA_DIAG = 16
SB_HEADS = 8
SB_HEAD_DIM = 128
LRU_BLOCKS = 8
LRU_BLOCK_DIM = 128
LRU_CONV = 4
LRU_C = 8.0
CONV_KERNEL = 31
LANES = 128
SUBLANES = 8

OFF_GQ, OFF_GK, OFF_GV, OFF_GG = 0, 512, 1024, 2048
N_PROJ_GLA = 3072
OFF_SQ, OFF_SK, OFF_SV = 0, 1024, 2048
OFF_LX, OFF_LG, OFF_CU = 3072, 4096, 5120
N_PROJ_REST = 7168
SB_DEAD_LOG = -105.0

VMEM_SLACK = 6 << 20


def _params(sem, vmem_bytes):
    return pltpu.CompilerParams(dimension_semantics=sem, vmem_limit_bytes=int(vmem_bytes))


def _rmsnorm_kernel(x_ref, g_ref, o_ref, *, eps):
    x = x_ref[...]
    ms = jnp.mean(x * x, axis=-1, keepdims=True)
    o_ref[...] = ((x * lax.rsqrt(ms + eps)) * g_ref[...]).astype(o_ref.dtype)


def rmsnorm(x, g, out_dtype, tm=256, eps=1e-6):
    m, d = x.shape
    return pl.pallas_call(
        functools.partial(_rmsnorm_kernel, eps=eps),
        grid=(m // tm,),
        in_specs=[pl.BlockSpec((tm, d), lambda i: (i, 0)),
                  pl.BlockSpec((1, d), lambda i: (0, 0))],
        out_specs=pl.BlockSpec((tm, d), lambda i: (i, 0)),
        out_shape=jax.ShapeDtypeStruct((m, d), out_dtype),
        compiler_params=_params(("parallel",), 4 * tm * d * 4 + VMEM_SLACK),
        name="rmsnorm",
    )(x, g.reshape(1, d))


def _matmul_kernel(*refs, n_a, has_res, nk):
    a_refs = refs[:n_a]
    w_ref = refs[n_a]
    res_ref = refs[n_a + 1] if has_res else None
    o_ref = refs[n_a + 1 + has_res]
    acc_ref = refs[n_a + 2 + has_res] if nk > 1 else None

    ka = a_refs[0].shape[1]
    part = None
    for g, a_ref in enumerate(a_refs):
        w = w_ref[...] if n_a == 1 else w_ref[g * ka:(g + 1) * ka, :]
        d = jnp.dot(a_ref[...], w, preferred_element_type=F32)
        part = d if part is None else part + d

    def finish(acc):
        if has_res:
            acc = acc + res_ref[...]
        o_ref[...] = acc.astype(o_ref.dtype)

    if nk == 1:
        finish(part)
    else:
        k = pl.program_id(2)

        @pl.when(k == 0)
        def _():
            acc_ref[...] = part

        @pl.when(k > 0)
        def _():
            acc_ref[...] += part

        @pl.when(k == nk - 1)
        def _():
            finish(acc_ref[...])


def matmul(a_list, w, res=None, *, out_dtype, tm, tn, tk=None):
    n_a = len(a_list)
    m, ka = a_list[0].shape
    k_total, n = w.shape
    assert ka * n_a == k_total
    tk = ka if tk is None else tk
    assert n_a == 1 or tk == ka
    nk = ka // tk
    assert ka % tk == 0 and m % tm == 0 and n % tn == 0
    in_specs = [pl.BlockSpec((tm, tk), lambda i, j, k: (i, k)) for _ in a_list]
    in_specs.append(pl.BlockSpec((tk * n_a, tn), lambda i, j, k: (k, j)))
    args = list(a_list) + [w]
    if res is not None:
        in_specs.append(pl.BlockSpec((tm, tn), lambda i, j, k: (i, j)))
        args.append(res)
    scratch = [pltpu.VMEM((tm, tn), F32)] if nk > 1 else []
    osz = jnp.dtype(out_dtype).itemsize
    vmem = (2 * (n_a * tm * tk * 2 + tk * n_a * tn * 2 + tm * tn * osz
                 + (tm * tn * 4 if res is not None else 0))
            + (tm * tn * 4 if nk > 1 else 0) + tm * tn * 4 + VMEM_SLACK)
    return pl.pallas_call(
        functools.partial(_matmul_kernel, n_a=n_a, has_res=res is not None, nk=nk),
        grid=(m // tm, n // tn, nk),
        in_specs=in_specs,
        out_specs=pl.BlockSpec((tm, tn), lambda i, j, k: (i, j)),
        out_shape=jax.ShapeDtypeStruct((m, n), out_dtype),
        scratch_shapes=scratch,
        compiler_params=_params(("parallel", "parallel", "arbitrary"), vmem),
        name="matmul",
    )(*args)


CAST_ROWS = 512


def _ws_matmul_kernel(*refs, n_a, n_w, transposed, shift, has_res, has_side, swiglu):
    a_refs = refs[:n_a]
    w_refs = refs[n_a:n_a + n_w]
    pos = n_a + n_w
    x_refs = refs[pos:pos + n_w] if shift else (None,) * n_w
    pos += n_w if shift else 0
    res_ref = refs[pos] if has_res else None
    pos += has_res
    side_ref = refs[pos] if has_side else None
    pos += has_side
    o_ref = refs[pos]
    pos += 1
    if has_side:
        refs[pos][...] = side_ref[...].astype(BF16)
        pos += 1
    wb_refs = refs[pos:pos + n_w]

    @pl.when(pl.program_id(1) == 0)
    def _():
        for w_ref, x_ref, wb_ref in zip(w_refs, x_refs, wb_refs):
            k = wb_ref.shape[0]
            for r in range(0, k, CAST_ROWS):
                if not transposed:
                    w = w_ref[r:r + CAST_ROWS, :]
                elif shift:
                    w = jnp.concatenate([w_ref[shift:, r:r + CAST_ROWS],
                                         x_ref[:, r:r + CAST_ROWS]], axis=0).T
                else:
                    w = w_ref[:, r:r + CAST_ROWS].T
                wb_ref[r:r + CAST_ROWS, :] = w.astype(BF16)

    ka = a_refs[0].shape[1]
    outs = []
    for wb_ref in wb_refs:
        acc = None
        for g, a_ref in enumerate(a_refs):
            d = jnp.dot(a_ref[...], wb_ref[g * ka:(g + 1) * ka, :], preferred_element_type=F32)
            acc = d if acc is None else acc + d
        outs.append(acc)
    if swiglu:
        gate, up = outs
        out = (gate * jax.nn.sigmoid(gate)) * up
    else:
        out = outs[0]
    if has_res:
        out = out + res_ref[...]
    o_ref[...] = out.astype(o_ref.dtype)


def ws_matmul(a_list, w_list, layer, res=None, *, out_dtype, tm, tn, col0=0, ncols=None, shift=0,
              transposed=False, swiglu=False, side=None, name="ws_matmul"):
    n_a, n_w = len(a_list), len(w_list)
    m, ka = a_list[0].shape
    if transposed:
        _, n_total, k_total = w_list[0].shape
    else:
        _, k_total, n_total = w_list[0].shape
    ncols = n_total if ncols is None else ncols
    assert ka * n_a == k_total and m % tm == 0 and col0 % tn == 0
    assert k_total % CAST_ROWS == 0 and n_w == (2 if swiglu else 1)
    assert shift % SUBLANES == 0 and tn % shift == 0 if shift else True
    assert transposed or not shift
    assert ncols % tn == 0 or (col0 + ncols == n_total and not shift)
    ncb = pl.cdiv(ncols, tn)
    cb = col0 // tn
    in_specs = [pl.BlockSpec((tm, ka), lambda j, i: (i, 0)) for _ in a_list]
    if transposed:
        in_specs += [pl.BlockSpec((None, tn, k_total), lambda j, i: (layer, cb + j, 0)) for _ in w_list]
    else:
        in_specs += [pl.BlockSpec((None, k_total, tn), lambda j, i: (layer, 0, cb + j)) for _ in w_list]
    args = list(a_list) + list(w_list)
    if shift:
        per = tn // shift
        in_specs += [pl.BlockSpec((None, shift, k_total), lambda j, i: (layer, (cb + j + 1) * per, 0))
                     for _ in w_list]
        args += list(w_list)
    if res is not None:
        in_specs.append(pl.BlockSpec((tm, tn), lambda j, i: (i, j)))
        args.append(res)
    nm = m // tm
    out_specs = [pl.BlockSpec((tm, tn), lambda j, i: (i, j))]
    out_shape = [jax.ShapeDtypeStruct((m, ncols), out_dtype)]
    side_bytes = 0
    if side is not None:
        _, srows, scols = side.shape
        bf16_rows = 2 * SUBLANES
        slab = next(s for s in range(bf16_rows, srows + 1, bf16_rows)
                    if srows % s == 0 and srows // s <= ncb * nm)
        last = srows // slab - 1
        in_specs.append(pl.BlockSpec((None, slab, scols),
                                     lambda j, i: (layer, jnp.minimum(j * nm + i, last), 0)))
        args.append(side)
        out_specs.append(pl.BlockSpec((slab, scols), lambda j, i: (jnp.minimum(j * nm + i, last), 0)))
        out_shape.append(jax.ShapeDtypeStruct((srows, scols), BF16))
        side_bytes = slab * scols * 6
    osz = jnp.dtype(out_dtype).itemsize
    vmem = (2 * (n_a * tm * ka * 2 + n_w * k_total * (tn + shift) * 4
                 + tm * tn * osz + (tm * tn * 4 if res is not None else 0) + side_bytes)
            + n_w * k_total * tn * 2 + (n_w + 1) * tm * tn * 4 + VMEM_SLACK)
    outs = pl.pallas_call(
        functools.partial(_ws_matmul_kernel, n_a=n_a, n_w=n_w, transposed=transposed, shift=shift,
                          has_res=res is not None, has_side=side is not None, swiglu=swiglu),
        grid=(ncb, nm),
        in_specs=in_specs,
        out_specs=out_specs,
        out_shape=out_shape,
        scratch_shapes=[pltpu.VMEM((k_total, tn), BF16) for _ in w_list],
        compiler_params=_params(("parallel", "arbitrary"), vmem),
        name=name,
    )(*args)
    return outs[0] if side is None else tuple(outs)


def _softplus_parts(z):
    l = jnp.log(1.0 + jnp.exp(-jnp.abs(z)))
    return jnp.minimum(z, 0.0) - l, -jnp.maximum(z, 0.0) - l


def _split_bf16(x):
    hi = x.astype(BF16)
    lo = (x - hi.astype(F32)).astype(BF16)
    return hi, lo


def _dot_nt(a, b):
    return lax.dot_general(a, b, (((1,), (1,)), ((), ())), preferred_element_type=F32)


def _dot_tn(a, b):
    return lax.dot_general(a, b, (((0,), (0,)), ((), ())), preferred_element_type=F32)


def _iota2(shape, dim):
    return lax.broadcasted_iota(jnp.int32, shape, dim)


def _sb_kernel(q_ref, k_ref, v_ref, o_ref, *, tb, scale):
    i = pl.program_id(2)
    hd = SB_HEAD_DIM
    nh = q_ref.shape[1] // hd
    row = _iota2((tb, tb), 0)
    col = _iota2((tb, tb), 1)
    tri = jnp.where(row > col, 1.0, 0.0).astype(BF16)
    before = col < row
    qs = [q_ref[:, h * hd:(h + 1) * hd].astype(BF16) for h in range(nh)]

    def block(h, ks, carry, acc, diagonal):
        kb = k_ref[pl.ds(ks, tb), h * hd:(h + 1) * hd].astype(BF16)
        vb = v_ref[pl.ds(ks, tb), h * hd:(h + 1) * hd].astype(BF16)
        z = _dot_nt(qs[h], kb) * scale
        log_beta, log_rest = _softplus_parts(z)
        if diagonal:
            log_rest = jnp.where(before, log_rest, 0.0)
        hi, lo = _split_bf16(log_rest)
        tail = (jnp.dot(hi, tri, preferred_element_type=F32)
                + jnp.dot(lo, tri, preferred_element_type=F32))
        w = jnp.exp(log_beta + tail + carry)
        if diagonal:
            w = jnp.where(before, w, 0.0)
        acc = acc + jnp.dot(w.astype(BF16), vb, preferred_element_type=F32)
        carry = carry + jnp.sum(log_rest, axis=-1, keepdims=True)
        return carry, acc

    def alive(carries):
        top = carries[0]
        for c in carries[1:]:
            top = jnp.maximum(top, c)
        return (jnp.max(top) >= SB_DEAD_LOG).astype(jnp.int32)

    k0 = pl.multiple_of(i * tb, tb)
    state = [block(h, k0, jnp.zeros((tb, 1), F32), jnp.zeros((tb, hd), F32), True)
             for h in range(nh)]
    carries = tuple(s[0] for s in state)
    accs = tuple(s[1] for s in state)

    def cond(st):
        return (st[0] < i) & (st[1] > 0)

    def body(st):
        jb, _, carries, accs = st
        ks = pl.multiple_of((i - 1 - jb) * tb, tb)
        new = [block(h, ks, carries[h], accs[h], False) for h in range(nh)]
        carries = tuple(s[0] for s in new)
        return jb + 1, alive(carries), carries, tuple(s[1] for s in new)

    _, _, _, accs = lax.while_loop(cond, body, (jnp.int32(0), alive(carries), carries, accs))
    for h in range(nh):
        o_ref[:, h * hd:(h + 1) * hd] = accs[h].astype(o_ref.dtype)


def stick_breaking(proj, bsz, t, *, tb=256, heads_per_step=2):
    m = proj.shape[0]
    nq = t // tb
    wd = heads_per_step * SB_HEAD_DIM
    cq, ck, cv = OFF_SQ // wd, OFF_SK // wd, OFF_SV // wd
    vmem = (2 * (tb * wd * 4 + 2 * t * wd * 4 + tb * wd * 2)
            + heads_per_step * 12 * tb * tb * 4 + VMEM_SLACK)
    return pl.pallas_call(
        functools.partial(_sb_kernel, tb=tb, scale=SB_HEAD_DIM ** -0.5),
        grid=(bsz, SB_HEADS // heads_per_step, nq),
        in_specs=[pl.BlockSpec((tb, wd), lambda b, h, i: (b * nq + i, cq + h)),
                  pl.BlockSpec((t, wd), lambda b, h, i: (b, ck + h)),
                  pl.BlockSpec((t, wd), lambda b, h, i: (b, cv + h))],
        out_specs=pl.BlockSpec((tb, wd), lambda b, h, i: (b * nq + i, h)),
        out_shape=jax.ShapeDtypeStruct((m, GROUP_WIDTH), BF16),
        compiler_params=_params(("parallel", "parallel", "arbitrary"), vmem),
        name="stick_breaking",
    )(proj, proj, proj)


def _gla_kernel(gq_ref, gk_ref, gv_ref, gg_ref, glr_ref, w2_ref, bg_ref, ng_ref, o_ref, st_ref,
                *, tt, eps):
    @pl.when(pl.program_id(1) == 0)
    def _():
        st_ref[...] = jnp.zeros_like(st_ref)

    row = _iota2((tt, tt), 0)
    col = _iota2((tt, tt), 1)
    cmr = col - row

    gate = jnp.dot(glr_ref[...].astype(BF16), w2_ref[...], preferred_element_type=F32) + bg_ref[...]
    log_alpha = _softplus_parts(gate)[0] * (1.0 / GLA_GATE_TAU)
    la_hi, la_lo = _split_bf16(log_alpha)

    def rowsum(sel):
        s = sel.astype(BF16)
        return (jnp.dot(s, la_hi, preferred_element_type=F32)
                + jnp.dot(s, la_lo, preferred_element_type=F32))

    cum = rowsum(jnp.where(col <= row, 1.0, 0.0))
    last = cum[tt - 1:tt, :]

    levels = []
    s = tt // 2
    while s >= GLA_DIAG:
        start_r = (row & ~(2 * s - 1)) + s
        is_r = (row & s) != 0
        sel = jnp.where(is_r,
                        jnp.where((col >= start_r) & (col <= row), 1.0, 0.0),
                        jnp.where((col > row) & (col < start_r), 1.0, 0.0))
        same_pair = (row & ~(2 * s - 1)) == (col & ~(2 * s - 1))
        levels.append((s, jnp.exp(rowsum(sel)), same_pair))
        s //= 2

    q_all = gq_ref[...] * (GLA_DK ** -0.5)
    k_all = gk_ref[...]
    v_all = gv_ref[...]
    rowl = _iota2((tt, GLA_DK), 0)
    dv = GLA_DV
    for h in range(GLA_HEADS):
        ksl = slice(h * GLA_DK, (h + 1) * GLA_DK)
        vsl = slice(h * dv, (h + 1) * dv)
        q, k, c = q_all[:, ksl], k_all[:, ksl], cum[:, ksl]
        v_bf = v_all[:, vsl].astype(BF16)
        st = st_ref[h]

        o = _dot_nt((q * jnp.exp(c)).astype(BF16), st.astype(BF16))

        scores = jnp.zeros((tt, tt), F32)
        for s, decay, same_pair in levels:
            d = decay[:, ksl]
            is_r = (rowl & s) != 0
            qh = jnp.where(is_r, q * d, 0.0).astype(BF16)
            kh = jnp.where(is_r, 0.0, k * d).astype(BF16)
            sc = _dot_nt(qh, kh)
            scores = scores + (sc if 2 * s == tt else jnp.where(same_pair, sc, 0.0))

        for dd in range(GLA_DIAG):
            k_sh = k if dd == 0 else pltpu.roll(k, dd, 0)
            c_sh = c if dd == 0 else pltpu.roll(c, dd, 0)
            e = jnp.exp(jnp.minimum(c - c_sh, 0.0))
            sd = jnp.sum(q * k_sh * e, axis=-1, keepdims=True)
            ok = (cmr == -dd) & ((row & (GLA_DIAG - 1)) >= dd)
            scores = scores + jnp.where(ok, sd, 0.0)

        o = o + jnp.dot(scores.astype(BF16), v_bf, preferred_element_type=F32)

        lh = last[:, ksl]
        k_dec = (k * jnp.exp(lh - c)).astype(BF16)
        st_ref[h] = st * jnp.exp(lh) + _dot_tn(v_bf, k_dec)

        ms = jnp.mean(o * o, axis=-1, keepdims=True)
        y = (o * lax.rsqrt(ms + eps)) * ng_ref[...]
        g_out = gg_ref[:, vsl]
        o_ref[:, vsl] = (y * (g_out * jax.nn.sigmoid(g_out))).astype(o_ref.dtype)


def gla(proj, glr, w2, b_gate, norm_g, bsz, t, *, tt=128, eps=1e-6):
    m = proj.shape[0]
    nt = t // tt
    hk = GLA_HEADS * GLA_DK
    row = lambda b, i: b * nt + i
    vmem = (2 * (2 * tt * hk * 4 + 2 * tt * GROUP_WIDTH * 4 + tt * LANES * 4 + tt * GROUP_WIDTH * 2)
            + GLA_HEADS * GLA_DV * GLA_DK * 4 + 24 * tt * hk * 4 + VMEM_SLACK)
    return pl.pallas_call(
        functools.partial(_gla_kernel, tt=tt, eps=eps),
        grid=(bsz, nt),
        in_specs=[pl.BlockSpec((tt, hk), lambda b, i: (row(b, i), OFF_GQ // hk)),
                  pl.BlockSpec((tt, hk), lambda b, i: (row(b, i), OFF_GK // hk)),
                  pl.BlockSpec((tt, GROUP_WIDTH), lambda b, i: (row(b, i), OFF_GV // GROUP_WIDTH)),
                  pl.BlockSpec((tt, GROUP_WIDTH), lambda b, i: (row(b, i), OFF_GG // GROUP_WIDTH)),
                  pl.BlockSpec((tt, LANES), lambda b, i: (row(b, i), 0)),
                  pl.BlockSpec((LANES, hk), lambda b, i: (0, 0)),
                  pl.BlockSpec((1, hk), lambda b, i: (0, 0)),
                  pl.BlockSpec((1, GLA_DV), lambda b, i: (0, 0))],
        out_specs=pl.BlockSpec((tt, GROUP_WIDTH), lambda b, i: (row(b, i), 0)),
        out_shape=jax.ShapeDtypeStruct((m, GROUP_WIDTH), BF16),
        scratch_shapes=[pltpu.VMEM((GLA_HEADS, GLA_DV, GLA_DK), F32)],
        compiler_params=_params(("parallel", "arbitrary"), vmem),
        name="gla",
    )(proj, proj, proj, proj, glr, w2, b_gate.reshape(1, hk), norm_g.reshape(1, GLA_DV))


def _lru_kernel(lx_ref, lg_ref, cw_ref, cb_ref, wax_ref, ba_ref, bx_ref, lam_ref, o_ref,
                ext_ref, h_ref, *, tt):
    pad = SUBLANES

    @pl.when(pl.program_id(1) == 0)
    def _():
        ext_ref[0:pad, :] = jnp.zeros((pad, ext_ref.shape[1]), F32)
        h_ref[...] = jnp.zeros_like(h_ref)

    ext_ref[pad:pad + tt, :] = lx_ref[...]
    xc = cb_ref[...] + cw_ref[0:1, :] * ext_ref[pl.ds(pad - LRU_CONV + 1, tt), :]
    for kk in range(1, LRU_CONV):
        xc = xc + cw_ref[kk:kk + 1, :] * ext_ref[pl.ds(pad - LRU_CONV + 1 + kk, tt), :]
    ext_ref[0:pad, :] = ext_ref[tt:tt + pad, :]

    bd = LRU_BLOCK_DIM
    r_parts, i_parts = [], []
    for n in range(LRU_BLOCKS):
        ri = jnp.dot(xc[:, n * bd:(n + 1) * bd].astype(BF16), wax_ref[n], preferred_element_type=F32)
        r_parts.append(ri[:, :bd])
        i_parts.append(ri[:, bd:])
    r = jax.nn.sigmoid(jnp.concatenate(r_parts, axis=1) + ba_ref[...])
    ig = jax.nn.sigmoid(jnp.concatenate(i_parts, axis=1) + bx_ref[...])

    lam = lam_ref[...]
    softplus_neg_lam = jnp.maximum(-lam, 0.0) + jnp.log1p(jnp.exp(-jnp.abs(lam)))
    log_a = (-LRU_C) * r * softplus_neg_lam
    a = jnp.exp(log_a)
    u = jnp.sqrt(-jnp.tanh(log_a) * (a * a + 1.0)) * (ig * xc)

    rowi = _iota2(a.shape, 0)
    sft = 1
    while sft < tt:
        keep = rowi >= sft
        a_prev = jnp.where(keep, pltpu.roll(a, sft, 0), 1.0)
        u_prev = jnp.where(keep, pltpu.roll(u, sft, 0), 0.0)
        u = a * u_prev + u
        a = a * a_prev
        sft *= 2
    h = u + a * h_ref[0:1, :]
    h_ref[0:1, :] = h[tt - 1:tt, :]

    lg = lg_ref[...]
    gelu = 0.5 * lg * (1.0 + jnp.tanh(0.7978845608028654 * (lg + 0.044715 * (lg * lg * lg))))
    o_ref[...] = (h * gelu).astype(o_ref.dtype)


def rglru(proj, conv_w, conv_b, wax, b_a, b_x, lam, bsz, t, *, tt=256):
    m = proj.shape[0]
    nt = t // tt
    w = GROUP_WIDTH
    row = lambda b, i: b * nt + i
    vec = lambda: pl.BlockSpec((1, w), lambda b, i: (0, 0))
    vmem = 2 * (2 * tt * w * 4 + tt * w * 2) + 24 * tt * w * 4 + VMEM_SLACK
    return pl.pallas_call(
        functools.partial(_lru_kernel, tt=tt),
        grid=(bsz, nt),
        in_specs=[pl.BlockSpec((tt, w), lambda b, i: (row(b, i), OFF_LX // w)),
                  pl.BlockSpec((tt, w), lambda b, i: (row(b, i), OFF_LG // w)),
                  pl.BlockSpec((LRU_CONV, w), lambda b, i: (0, 0)),
                  vec(),
                  pl.BlockSpec((LRU_BLOCKS, LRU_BLOCK_DIM, 2 * LRU_BLOCK_DIM), lambda b, i: (0, 0, 0)),
                  vec(), vec(), vec()],
        out_specs=pl.BlockSpec((tt, w), lambda b, i: (row(b, i), 0)),
        out_shape=jax.ShapeDtypeStruct((m, w), BF16),
        scratch_shapes=[pltpu.VMEM((tt + SUBLANES, w), F32), pltpu.VMEM((SUBLANES, w), F32)],
        compiler_params=_params(("parallel", "arbitrary"), vmem),
        name="rglru",
    )(proj, proj, conv_w, conv_b.reshape(1, w), wax, b_a.reshape(1, w), b_x.reshape(1, w),
      lam.reshape(1, w))


CONF_PAD = 32


def _conformer_kernel(val_ref, gte_ref, cw_ref, cb_ref, g_ref, b_ref, o_ref, ext_ref, sh_ref,
                      *, tt, eps):
    pad = CONF_PAD
    rows = tt + pad

    @pl.when(pl.program_id(1) == 0)
    def _():
        ext_ref[0:pad, :] = jnp.zeros((pad, ext_ref.shape[1]), F32)

    ext_ref[pad:pad + tt, :] = val_ref[...] * jax.nn.sigmoid(gte_ref[...])
    ext = ext_ref[...]
    for b in range(1, SUBLANES):
        sh_ref[b - 1] = pltpu.roll(ext, rows - b, 0)
    base = pad - CONV_KERNEL + 1
    y = cb_ref[...]
    for kk in range(CONV_KERNEL):
        a, b = divmod(base + kk, SUBLANES)
        src = ext_ref if b == 0 else sh_ref.at[b - 1]
        y = y + cw_ref[kk:kk + 1, :] * src[pl.ds(a * SUBLANES, tt), :]
    ext_ref[0:pad, :] = ext_ref[tt:tt + pad, :]

    mu = jnp.mean(y, axis=-1, keepdims=True)
    yc = y - mu
    var = jnp.mean(yc * yc, axis=-1, keepdims=True)
    z = (yc * lax.rsqrt(var + eps)) * g_ref[...] + b_ref[...]
    o_ref[...] = (z * jax.nn.sigmoid(z)).astype(o_ref.dtype)


def conformer(proj, conv_w, conv_b, ln_g, ln_b, bsz, t, *, tt=256, eps=1e-5):
    m = proj.shape[0]
    nt = t // tt
    w = GROUP_WIDTH
    row = lambda b, i: b * nt + i
    vec = lambda: pl.BlockSpec((1, w), lambda b, i: (0, 0))
    vmem = (2 * (2 * tt * w * 4 + tt * w * 2) + (8 * tt + SUBLANES * (tt + CONF_PAD)) * w * 4
            + VMEM_SLACK)
    return pl.pallas_call(
        functools.partial(_conformer_kernel, tt=tt, eps=eps),
        grid=(bsz, nt),
        in_specs=[pl.BlockSpec((tt, w), lambda b, i: (row(b, i), OFF_CU // w)),
                  pl.BlockSpec((tt, w), lambda b, i: (row(b, i), OFF_CU // w + 1)),
                  pl.BlockSpec((CONV_KERNEL, w), lambda b, i: (0, 0)),
                  vec(), vec(), vec()],
        out_specs=pl.BlockSpec((tt, w), lambda b, i: (row(b, i), 0)),
        out_shape=jax.ShapeDtypeStruct((m, w), BF16),
        scratch_shapes=[pltpu.VMEM((tt + CONF_PAD, w), F32),
                        pltpu.VMEM((SUBLANES - 1, tt + CONF_PAD, w), F32)],
        compiler_params=_params(("parallel", "arbitrary"), vmem),
        name="conformer",
    )(proj, proj, conv_w, conv_b.reshape(1, w), ln_g.reshape(1, w), ln_b.reshape(1, w))


def kernel(x, norm_mix_g, w_in, gla_w_gate2, gla_b_gate, gla_norm_g, lru_conv_w, lru_conv_b, lru_w_a, lru_b_a, lru_w_x, lru_b_x, lru_lambda, conf_conv_w, conf_conv_b, conf_ln_g, conf_ln_b, w_out, norm_ffn_g, ffn_w_gate, ffn_w_up, ffn_w_down, final_norm_g):
    bsz, t, d = x.shape
    depth = w_in.shape[0]
    xf = x.reshape(bsz * t, d)
    w_in_t = jnp.swapaxes(w_in, 1, 2)
    for l in range(depth):
        rank0 = N_PROJ_GLA
        w2 = jnp.pad(gla_w_gate2[l], ((0, LANES - GLA_GATE_RANK), (0, 0))).astype(BF16)
        wax = jnp.concatenate([lru_w_a[l], lru_w_x[l]], axis=-1).astype(BF16)

        h = rmsnorm(xf, norm_mix_g[l], BF16)
        proj_gla = ws_matmul([h], [w_in_t], l, out_dtype=F32, tm=1024, tn=512, ncols=N_PROJ_GLA,
                             transposed=True, name="in_proj_gla")
        proj = ws_matmul([h], [w_in_t], l, out_dtype=F32, tm=1024, tn=512, col0=rank0,
                         ncols=N_PROJ_REST, shift=GLA_GATE_RANK, transposed=True, name="in_proj_rest")
        glr = ws_matmul([h], [w_in_t], l, out_dtype=F32, tm=1024, tn=LANES, col0=rank0, ncols=LANES,
                        transposed=True, name="in_proj_rank")
        o_a = gla(proj_gla, glr, w2, gla_b_gate[l], gla_norm_g[l], bsz, t)
        o_b = stick_breaking(proj, bsz, t)
        o_c = rglru(proj, lru_conv_w[l], lru_conv_b[l], wax, lru_b_a[l], lru_b_x[l], lru_lambda[l], bsz, t)
        o_d = conformer(proj, conf_conv_w[l], conf_conv_b[l], conf_ln_g[l], conf_ln_b[l], bsz, t)
        xf = ws_matmul([o_a, o_b, o_c, o_d], [w_out], l, xf, out_dtype=F32, tm=1024, tn=512,
                       name="out_proj")

        h = rmsnorm(xf, norm_ffn_g[l], BF16)
        act, w_down = ws_matmul([h], [ffn_w_gate, ffn_w_up], l, out_dtype=BF16, tm=512, tn=512,
                                swiglu=True, side=ffn_w_down, name="ffn_gate_up")
        xf = matmul([act], w_down, xf, out_dtype=F32, tm=512, tn=512)
    return rmsnorm(xf, final_norm_g, F32).reshape(bsz, t, d)
```

```python
import functools

import jax
import jax.numpy as jnp
from jax import lax
from jax.experimental import pallas as pl
from jax.experimental.pallas import tpu as pltpu

F32 = jnp.float32
BF16 = jnp.bfloat16

GROUP_WIDTH = 1024
GLA_HEADS = 4
GLA_DK = 128
GLA_DV = 256
GLA_GATE_RANK = 16
GLA_GATE_TAU = 16.0
GLA_DIAG = 16
SB_HEADS = 8
SB_HEAD_DIM = 128
LRU_BLOCKS = 8
LRU_BLOCK_DIM = 128
LRU_CONV = 4
LRU_C = 8.0
CONV_KERNEL = 31
LANES = 128
SUBLANES = 8

OFF_GQ, OFF_GK, OFF_GV, OFF_GG = 0, 512, 1024, 2048
N_PROJ_GLA = 3072
OFF_SQ, OFF_SK, OFF_SV = 0, 1024, 2048
OFF_LX, OFF_LG, OFF_CU = 3072, 4096, 5120
N_PROJ_REST = 7168
SB_DEAD_LOG = -105.0

VMEM_SLACK = 6 << 20


def _params(sem, vmem_bytes):
    return pltpu.CompilerParams(dimension_semantics=sem, vmem_limit_bytes=int(vmem_bytes))


def _rmsnorm_kernel(x_ref, g_ref, o_ref, *, eps):
    x = x_ref[...]
    ms = jnp.mean(x * x, axis=-1, keepdims=True)
    o_ref[...] = ((x * lax.rsqrt(ms + eps)) * g_ref[...]).astype(o_ref.dtype)


def rmsnorm(x, g, out_dtype, tm=256, eps=1e-6):
    m, d = x.shape
    return pl.pallas_call(
        functools.partial(_rmsnorm_kernel, eps=eps),
        grid=(m // tm,),
        in_specs=[pl.BlockSpec((tm, d), lambda i: (i, 0)),
                  pl.BlockSpec((1, d), lambda i: (0, 0))],
        out_specs=pl.BlockSpec((tm, d), lambda i: (i, 0)),
        out_shape=jax.ShapeDtypeStruct((m, d), out_dtype),
        compiler_params=_params(("parallel",), 4 * tm * d * 4 + VMEM_SLACK),
        name="rmsnorm",
    )(x, g.reshape(1, d))


def _matmul_kernel(*refs, n_a, has_res, nk):
    a_refs = refs[:n_a]
    w_ref = refs[n_a]
    res_ref = refs[n_a + 1] if has_res else None
    o_ref = refs[n_a + 1 + has_res]
    acc_ref = refs[n_a + 2 + has_res] if nk > 1 else None

    ka = a_refs[0].shape[1]
    part = None
    for g, a_ref in enumerate(a_refs):
        w = w_ref[...] if n_a == 1 else w_ref[g * ka:(g + 1) * ka, :]
        d = jnp.dot(a_ref[...], w, preferred_element_type=F32)
        part = d if part is None else part + d

    def finish(acc):
        if has_res:
            acc = acc + res_ref[...]
        o_ref[...] = acc.astype(o_ref.dtype)

    if nk == 1:
        finish(part)
    else:
        k = pl.program_id(2)

        @pl.when(k == 0)
        def _():
            acc_ref[...] = part

        @pl.when(k > 0)
        def _():
            acc_ref[...] += part

        @pl.when(k == nk - 1)
        def _():
            finish(acc_ref[...])


def matmul(a_list, w, res=None, *, out_dtype, tm, tn, tk=None):
    n_a = len(a_list)
    m, ka = a_list[0].shape
    k_total, n = w.shape
    assert ka * n_a == k_total
    tk = ka if tk is None else tk
    assert n_a == 1 or tk == ka
    nk = ka // tk
    assert ka % tk == 0 and m % tm == 0 and n % tn == 0
    in_specs = [pl.BlockSpec((tm, tk), lambda i, j, k: (i, k)) for _ in a_list]
    in_specs.append(pl.BlockSpec((tk * n_a, tn), lambda i, j, k: (k, j)))
    args = list(a_list) + [w]
    if res is not None:
        in_specs.append(pl.BlockSpec((tm, tn), lambda i, j, k: (i, j)))
        args.append(res)
    scratch = [pltpu.VMEM((tm, tn), F32)] if nk > 1 else []
    osz = jnp.dtype(out_dtype).itemsize
    vmem = (2 * (n_a * tm * tk * 2 + tk * n_a * tn * 2 + tm * tn * osz
                 + (tm * tn * 4 if res is not None else 0))
            + (tm * tn * 4 if nk > 1 else 0) + tm * tn * 4 + VMEM_SLACK)
    return pl.pallas_call(
        functools.partial(_matmul_kernel, n_a=n_a, has_res=res is not None, nk=nk),
        grid=(m // tm, n // tn, nk),
        in_specs=in_specs,
        out_specs=pl.BlockSpec((tm, tn), lambda i, j, k: (i, j)),
        out_shape=jax.ShapeDtypeStruct((m, n), out_dtype),
        scratch_shapes=scratch,
        compiler_params=_params(("parallel", "parallel", "arbitrary"), vmem),
        name="matmul",
    )(*args)


def _ws_matmul_kernel(*refs, n_a, n_w, ncb, kc, transposed, shift, has_res, has_side, swiglu):
    a_refs = refs[:n_a]
    w_refs = refs[n_a:n_a + n_w]
    pos = n_a + n_w
    x_refs = refs[pos:pos + n_w] if shift else (None,) * n_w
    pos += n_w if shift else 0
    res_ref = refs[pos] if has_res else None
    pos += has_res
    side_ref = refs[pos] if has_side else None
    pos += has_side
    o_ref = refs[pos]
    pos += 1
    if has_side:
        refs[pos][...] = side_ref[...].astype(BF16)
        pos += 1
    wb_slots = (refs[pos:pos + n_w], refs[pos + n_w:pos + 2 * n_w])
    j = pl.program_id(0)
    i = pl.program_id(1)

    def convert_chunk(slot):
        rows = pl.ds(pl.multiple_of(i * kc, kc), kc)
        for w_ref, x_ref, wb_ref in zip(w_refs, x_refs, wb_slots[slot]):
            if not transposed:
                w = w_ref[...]
            elif shift:
                w = jnp.concatenate([w_ref[shift:, :], x_ref[...]], axis=0).T
            else:
                w = w_ref[...].T
            wb_ref[rows, :] = w.astype(BF16)

    def multiply(slot):
        ka = a_refs[0].shape[1]
        outs = []
        for wb_ref in wb_slots[slot]:
            acc = None
            for g, a_ref in enumerate(a_refs):
                d = jnp.dot(a_ref[...], wb_ref[g * ka:(g + 1) * ka, :], preferred_element_type=F32)
                acc = d if acc is None else acc + d
            outs.append(acc)
        if swiglu:
            gate, up = outs
            out = (gate * jax.nn.sigmoid(gate)) * up
        else:
            out = outs[0]
        if has_res:
            out = out + res_ref[...]
        o_ref[...] = out.astype(o_ref.dtype)

    @pl.when(j == 0)
    def _():
        convert_chunk(0)

    for parity in (0, 1):
        @pl.when((j > 0) & (j % 2 == parity))
        def _():
            convert_chunk(parity)
            multiply(1 - parity)


def ws_matmul(a_list, w_list, layer, res=None, *, out_dtype, tm, tn, col0=0, ncols=None, shift=0,
              transposed=False, swiglu=False, side=None, name="ws_matmul"):
    n_a, n_w = len(a_list), len(w_list)
    m, ka = a_list[0].shape
    if transposed:
        _, n_total, k_total = w_list[0].shape
    else:
        _, k_total, n_total = w_list[0].shape
    ncols = n_total if ncols is None else ncols
    nm = m // tm
    kc = k_total // nm
    assert ka * n_a == k_total and m % tm == 0 and col0 % tn == 0 and n_w == (2 if swiglu else 1)
    assert kc * nm == k_total and kc % LANES == 0
    assert shift % SUBLANES == 0 and tn % shift == 0 if shift else True
    assert transposed or not shift
    assert ncols % tn == 0 or (col0 + ncols == n_total and not shift)
    ncb = pl.cdiv(ncols, tn)
    cb = col0 // tn

    row_blk = lambda j, i: jnp.where(j > 0, i, 0)
    out_col = lambda j: jnp.maximum(j - 1, 0)
    w_col = lambda j: cb + jnp.minimum(j, ncb - 1)
    w_chunk = lambda j, i: jnp.where(j < ncb, i, nm - 1)

    in_specs = [pl.BlockSpec((tm, ka), lambda j, i: (row_blk(j, i), 0)) for _ in a_list]
    if transposed:
        in_specs += [pl.BlockSpec((None, tn, kc), lambda j, i: (layer, w_col(j), w_chunk(j, i)))
                     for _ in w_list]
    else:
        in_specs += [pl.BlockSpec((None, kc, tn), lambda j, i: (layer, w_chunk(j, i), w_col(j)))
                     for _ in w_list]
    args = list(a_list) + list(w_list)
    if shift:
        per = tn // shift
        in_specs += [pl.BlockSpec((None, shift, kc),
                                  lambda j, i: (layer, (w_col(j) + 1) * per, w_chunk(j, i)))
                     for _ in w_list]
        args += list(w_list)
    if res is not None:
        in_specs.append(pl.BlockSpec((tm, tn), lambda j, i: (row_blk(j, i), out_col(j))))
        args.append(res)
    out_specs = [pl.BlockSpec((tm, tn), lambda j, i: (row_blk(j, i), out_col(j)))]
    out_shape = [jax.ShapeDtypeStruct((m, ncols), out_dtype)]
    side_bytes = 0
    if side is not None:
        _, srows, scols = side.shape
        bf16_rows = 2 * SUBLANES
        slab = next(s for s in range(bf16_rows, srows + 1, bf16_rows)
                    if srows % s == 0 and srows // s <= (ncb + 1) * nm)
        last = srows // slab - 1
        in_specs.append(pl.BlockSpec((None, slab, scols),
                                     lambda j, i: (layer, jnp.minimum(j * nm + i, last), 0)))
        args.append(side)
        out_specs.append(pl.BlockSpec((slab, scols), lambda j, i: (jnp.minimum(j * nm + i, last), 0)))
        out_shape.append(jax.ShapeDtypeStruct((srows, scols), BF16))
        side_bytes = slab * scols * 6
    osz = jnp.dtype(out_dtype).itemsize
    vmem = (2 * (n_a * tm * ka * 2 + n_w * kc * (tn + shift) * 4
                 + tm * tn * osz + (tm * tn * 4 if res is not None else 0) + side_bytes)
            + 2 * n_w * k_total * tn * 2 + (n_w + 1) * tm * tn * 4 + VMEM_SLACK)
    outs = pl.pallas_call(
        functools.partial(_ws_matmul_kernel, n_a=n_a, n_w=n_w, ncb=ncb, kc=kc, transposed=transposed,
                          shift=shift, has_res=res is not None, has_side=side is not None,
                          swiglu=swiglu),
        grid=(ncb + 1, nm),
        in_specs=in_specs,
        out_specs=out_specs,
        out_shape=out_shape,
        scratch_shapes=[pltpu.VMEM((k_total, tn), BF16) for _ in range(2 * n_w)],
        compiler_params=_params(("arbitrary", "arbitrary"), vmem),
        name=name,
    )(*args)
    return outs[0] if side is None else tuple(outs)


def _softplus_parts(z):
    l = jnp.log(1.0 + jnp.exp(-jnp.abs(z)))
    return jnp.minimum(z, 0.0) - l, -jnp.maximum(z, 0.0) - l


def _split_bf16(x):
    hi = x.astype(BF16)
    lo = (x - hi.astype(F32)).astype(BF16)
    return hi, lo


def _dot_nt(a, b):
    return lax.dot_general(a, b, (((1,), (1,)), ((), ())), preferred_element_type=F32)


def _dot_tn(a, b):
    return lax.dot_general(a, b, (((0,), (0,)), ((), ())), preferred_element_type=F32)


def _iota2(shape, dim):
    return lax.broadcasted_iota(jnp.int32, shape, dim)


def _sb_kernel(q_ref, k_ref, v_ref, o_ref, *, tb, scale):
    i = pl.program_id(2)
    hd = SB_HEAD_DIM
    nh = q_ref.shape[1] // hd
    row = _iota2((tb, tb), 0)
    col = _iota2((tb, tb), 1)
    tri = jnp.where(row > col, 1.0, 0.0).astype(BF16)
    before = col < row
    qs = [q_ref[:, h * hd:(h + 1) * hd].astype(BF16) for h in range(nh)]

    def block(h, ks, carry, acc, diagonal):
        kb = k_ref[pl.ds(ks, tb), h * hd:(h + 1) * hd].astype(BF16)
        vb = v_ref[pl.ds(ks, tb), h * hd:(h + 1) * hd].astype(BF16)
        z = _dot_nt(qs[h], kb) * scale
        log_beta, log_rest = _softplus_parts(z)
        if diagonal:
            log_rest = jnp.where(before, log_rest, 0.0)
        hi, lo = _split_bf16(log_rest)
        tail = (jnp.dot(hi, tri, preferred_element_type=F32)
                + jnp.dot(lo, tri, preferred_element_type=F32))
        w = jnp.exp(log_beta + tail + carry)
        if diagonal:
            w = jnp.where(before, w, 0.0)
        acc = acc + jnp.dot(w.astype(BF16), vb, preferred_element_type=F32)
        carry = carry + jnp.sum(log_rest, axis=-1, keepdims=True)
        return carry, acc

    def alive(carries):
        top = carries[0]
        for c in carries[1:]:
            top = jnp.maximum(top, c)
        return (jnp.max(top) >= SB_DEAD_LOG).astype(jnp.int32)

    k0 = pl.multiple_of(i * tb, tb)
    state = [block(h, k0, jnp.zeros((tb, 1), F32), jnp.zeros((tb, hd), F32), True)
             for h in range(nh)]
    carries = tuple(s[0] for s in state)
    accs = tuple(s[1] for s in state)

    def cond(st):
        return (st[0] < i) & (st[1] > 0)

    def body(st):
        jb, _, carries, accs = st
        ks = pl.multiple_of((i - 1 - jb) * tb, tb)
        new = [block(h, ks, carries[h], accs[h], False) for h in range(nh)]
        carries = tuple(s[0] for s in new)
        return jb + 1, alive(carries), carries, tuple(s[1] for s in new)

    _, _, _, accs = lax.while_loop(cond, body, (jnp.int32(0), alive(carries), carries, accs))
    for h in range(nh):
        o_ref[:, h * hd:(h + 1) * hd] = accs[h].astype(o_ref.dtype)


def stick_breaking(proj, bsz, t, *, tb=256, heads_per_step=2):
    m = proj.shape[0]
    nq = t // tb
    wd = heads_per_step * SB_HEAD_DIM
    cq, ck, cv = OFF_SQ // wd, OFF_SK // wd, OFF_SV // wd
    vmem = (2 * (tb * wd * 4 + 2 * t * wd * 4 + tb * wd * 2)
            + heads_per_step * 12 * tb * tb * 4 + VMEM_SLACK)
    return pl.pallas_call(
        functools.partial(_sb_kernel, tb=tb, scale=SB_HEAD_DIM ** -0.5),
        grid=(bsz, SB_HEADS // heads_per_step, nq),
        in_specs=[pl.BlockSpec((tb, wd), lambda b, h, i: (b * nq + i, cq + h)),
                  pl.BlockSpec((t, wd), lambda b, h, i: (b, ck + h)),
                  pl.BlockSpec((t, wd), lambda b, h, i: (b, cv + h))],
        out_specs=pl.BlockSpec((tb, wd), lambda b, h, i: (b * nq + i, h)),
        out_shape=jax.ShapeDtypeStruct((m, GROUP_WIDTH), BF16),
        compiler_params=_params(("parallel", "parallel", "arbitrary"), vmem),
        name="stick_breaking",
    )(proj, proj, proj)


def _gla_kernel(gq_ref, gk_ref, gv_ref, gg_ref, glr_ref, w2_ref, bg_ref, ng_ref, o_ref, st_ref,
                *, tt, eps):
    @pl.when(pl.program_id(1) == 0)
    def _():
        st_ref[...] = jnp.zeros_like(st_ref)

    row = _iota2((tt, tt), 0)
    col = _iota2((tt, tt), 1)
    cmr = col - row

    gate = jnp.dot(glr_ref[...].astype(BF16), w2_ref[...], preferred_element_type=F32) + bg_ref[...]
    log_alpha = _softplus_parts(gate)[0] * (1.0 / GLA_GATE_TAU)
    la_hi, la_lo = _split_bf16(log_alpha)

    def rowsum(sel):
        s = sel.astype(BF16)
        return (jnp.dot(s, la_hi, preferred_element_type=F32)
                + jnp.dot(s, la_lo, preferred_element_type=F32))

    cum = rowsum(jnp.where(col <= row, 1.0, 0.0))
    last = cum[tt - 1:tt, :]

    levels = []
    s = tt // 2
    while s >= GLA_DIAG:
        start_r = (row & ~(2 * s - 1)) + s
        is_r = (row & s) != 0
        sel = jnp.where(is_r,
                        jnp.where((col >= start_r) & (col <= row), 1.0, 0.0),
                        jnp.where((col > row) & (col < start_r), 1.0, 0.0))
        same_pair = (row & ~(2 * s - 1)) == (col & ~(2 * s - 1))
        levels.append((s, jnp.exp(rowsum(sel)), same_pair))
        s //= 2

    q_all = gq_ref[...] * (GLA_DK ** -0.5)
    k_all = gk_ref[...]
    v_all = gv_ref[...]
    rowl = _iota2((tt, GLA_DK), 0)
    dv = GLA_DV
    for h in range(GLA_HEADS):
        ksl = slice(h * GLA_DK, (h + 1) * GLA_DK)
        vsl = slice(h * dv, (h + 1) * dv)
        q, k, c = q_all[:, ksl], k_all[:, ksl], cum[:, ksl]
        v_bf = v_all[:, vsl].astype(BF16)
        st = st_ref[h]

        o = _dot_nt((q * jnp.exp(c)).astype(BF16), st.astype(BF16))

        scores = jnp.zeros((tt, tt), F32)
        for s, decay, same_pair in levels:
            d = decay[:, ksl]
            is_r = (rowl & s) != 0
            qh = jnp.where(is_r, q * d, 0.0).astype(BF16)
            kh = jnp.where(is_r, 0.0, k * d).astype(BF16)
            sc = _dot_nt(qh, kh)
            scores = scores + (sc if 2 * s == tt else jnp.where(same_pair, sc, 0.0))

        for dd in range(GLA_DIAG):
            k_sh = k if dd == 0 else pltpu.roll(k, dd, 0)
            c_sh = c if dd == 0 else pltpu.roll(c, dd, 0)
            e = jnp.exp(jnp.minimum(c - c_sh, 0.0))
            sd = jnp.sum(q * k_sh * e, axis=-1, keepdims=True)
            ok = (cmr == -dd) & ((row & (GLA_DIAG - 1)) >= dd)
            scores = scores + jnp.where(ok, sd, 0.0)

        o = o + jnp.dot(scores.astype(BF16), v_bf, preferred_element_type=F32)

        lh = last[:, ksl]
        k_dec = (k * jnp.exp(lh - c)).astype(BF16)
        st_ref[h] = st * jnp.exp(lh) + _dot_tn(v_bf, k_dec)

        ms = jnp.mean(o * o, axis=-1, keepdims=True)
        y = (o * lax.rsqrt(ms + eps)) * ng_ref[...]
        g_out = gg_ref[:, vsl]
        o_ref[:, vsl] = (y * (g_out * jax.nn.sigmoid(g_out))).astype(o_ref.dtype)


def gla(proj, glr, w2, b_gate, norm_g, bsz, t, *, tt=128, eps=1e-6):
    m = proj.shape[0]
    nt = t // tt
    hk = GLA_HEADS * GLA_DK
    row = lambda b, i: b * nt + i
    vmem = (2 * (2 * tt * hk * 4 + 2 * tt * GROUP_WIDTH * 4 + tt * LANES * 4 + tt * GROUP_WIDTH * 2)
            + GLA_HEADS * GLA_DV * GLA_DK * 4 + 24 * tt * hk * 4 + VMEM_SLACK)
    return pl.pallas_call(
        functools.partial(_gla_kernel, tt=tt, eps=eps),
        grid=(bsz, nt),
        in_specs=[pl.BlockSpec((tt, hk), lambda b, i: (row(b, i), OFF_GQ // hk)),
                  pl.BlockSpec((tt, hk), lambda b, i: (row(b, i), OFF_GK // hk)),
                  pl.BlockSpec((tt, GROUP_WIDTH), lambda b, i: (row(b, i), OFF_GV // GROUP_WIDTH)),
                  pl.BlockSpec((tt, GROUP_WIDTH), lambda b, i: (row(b, i), OFF_GG // GROUP_WIDTH)),
                  pl.BlockSpec((tt, LANES), lambda b, i: (row(b, i), 0)),
                  pl.BlockSpec((LANES, hk), lambda b, i: (0, 0)),
                  pl.BlockSpec((1, hk), lambda b, i: (0, 0)),
                  pl.BlockSpec((1, GLA_DV), lambda b, i: (0, 0))],
        out_specs=pl.BlockSpec((tt, GROUP_WIDTH), lambda b, i: (row(b, i), 0)),
        out_shape=jax.ShapeDtypeStruct((m, GROUP_WIDTH), BF16),
        scratch_shapes=[pltpu.VMEM((GLA_HEADS, GLA_DV, GLA_DK), F32)],
        compiler_params=_params(("parallel", "arbitrary"), vmem),
        name="gla",
    )(proj, proj, proj, proj, glr, w2, b_gate.reshape(1, hk), norm_g.reshape(1, GLA_DV))


def _lru_kernel(lx_ref, lg_ref, cw_ref, cb_ref, wax_ref, ba_ref, bx_ref, lam_ref, o_ref,
                ext_ref, h_ref, *, tt):
    pad = SUBLANES

    @pl.when(pl.program_id(1) == 0)
    def _():
        ext_ref[0:pad, :] = jnp.zeros((pad, ext_ref.shape[1]), F32)
        h_ref[...] = jnp.zeros_like(h_ref)

    ext_ref[pad:pad + tt, :] = lx_ref[...]
    xc = cb_ref[...] + cw_ref[0:1, :] * ext_ref[pl.ds(pad - LRU_CONV + 1, tt), :]
    for kk in range(1, LRU_CONV):
        xc = xc + cw_ref[kk:kk + 1, :] * ext_ref[pl.ds(pad - LRU_CONV + 1 + kk, tt), :]
    ext_ref[0:pad, :] = ext_ref[tt:tt + pad, :]

    bd = LRU_BLOCK_DIM
    r_parts, i_parts = [], []
    for n in range(LRU_BLOCKS):
        ri = jnp.dot(xc[:, n * bd:(n + 1) * bd].astype(BF16), wax_ref[n], preferred_element_type=F32)
        r_parts.append(ri[:, :bd])
        i_parts.append(ri[:, bd:])
    r = jax.nn.sigmoid(jnp.concatenate(r_parts, axis=1) + ba_ref[...])
    ig = jax.nn.sigmoid(jnp.concatenate(i_parts, axis=1) + bx_ref[...])

    lam = lam_ref[...]
    softplus_neg_lam = jnp.maximum(-lam, 0.0) + jnp.log1p(jnp.exp(-jnp.abs(lam)))
    log_a = (-LRU_C) * r * softplus_neg_lam
    a = jnp.exp(log_a)
    u = jnp.sqrt(-jnp.tanh(log_a) * (a * a + 1.0)) * (ig * xc)

    rowi = _iota2(a.shape, 0)
    sft = 1
    while sft < tt:
        keep = rowi >= sft
        a_prev = jnp.where(keep, pltpu.roll(a, sft, 0), 1.0)
        u_prev = jnp.where(keep, pltpu.roll(u, sft, 0), 0.0)
        u = a * u_prev + u
        a = a * a_prev
        sft *= 2
    h = u + a * h_ref[0:1, :]
    h_ref[0:1, :] = h[tt - 1:tt, :]

    lg = lg_ref[...]
    gelu = 0.5 * lg * (1.0 + jnp.tanh(0.7978845608028654 * (lg + 0.044715 * (lg * lg * lg))))
    o_ref[...] = (h * gelu).astype(o_ref.dtype)


def rglru(proj, conv_w, conv_b, wax, b_a, b_x, lam, bsz, t, *, tt=256):
    m = proj.shape[0]
    nt = t // tt
    w = GROUP_WIDTH
    row = lambda b, i: b * nt + i
    vec = lambda: pl.BlockSpec((1, w), lambda b, i: (0, 0))
    vmem = 2 * (2 * tt * w * 4 + tt * w * 2) + 24 * tt * w * 4 + VMEM_SLACK
    return pl.pallas_call(
        functools.partial(_lru_kernel, tt=tt),
        grid=(bsz, nt),
        in_specs=[pl.BlockSpec((tt, w), lambda b, i: (row(b, i), OFF_LX // w)),
                  pl.BlockSpec((tt, w), lambda b, i: (row(b, i), OFF_LG // w)),
                  pl.BlockSpec((LRU_CONV, w), lambda b, i: (0, 0)),
                  vec(),
                  pl.BlockSpec((LRU_BLOCKS, LRU_BLOCK_DIM, 2 * LRU_BLOCK_DIM), lambda b, i: (0, 0, 0)),
                  vec(), vec(), vec()],
        out_specs=pl.BlockSpec((tt, w), lambda b, i: (row(b, i), 0)),
        out_shape=jax.ShapeDtypeStruct((m, w), BF16),
        scratch_shapes=[pltpu.VMEM((tt + SUBLANES, w), F32), pltpu.VMEM((SUBLANES, w), F32)],
        compiler_params=_params(("parallel", "arbitrary"), vmem),
        name="rglru",
    )(proj, proj, conv_w, conv_b.reshape(1, w), wax, b_a.reshape(1, w), b_x.reshape(1, w),
      lam.reshape(1, w))


CONF_PAD = 32


def _conformer_kernel(val_ref, gte_ref, cw_ref, cb_ref, g_ref, b_ref, o_ref, ext_ref, sh_ref,
                      *, tt, eps):
    pad = CONF_PAD
    rows = tt + pad

    @pl.when(pl.program_id(1) == 0)
    def _():
        ext_ref[0:pad, :] = jnp.zeros((pad, ext_ref.shape[1]), F32)

    ext_ref[pad:pad + tt, :] = val_ref[...] * jax.nn.sigmoid(gte_ref[...])
    ext = ext_ref[...]
    for b in range(1, SUBLANES):
        sh_ref[b - 1] = pltpu.roll(ext, rows - b, 0)
    base = pad - CONV_KERNEL + 1
    y = cb_ref[...]
    for kk in range(CONV_KERNEL):
        a, b = divmod(base + kk, SUBLANES)
        src = ext_ref if b == 0 else sh_ref.at[b - 1]
        y = y + cw_ref[kk:kk + 1, :] * src[pl.ds(a * SUBLANES, tt), :]
    ext_ref[0:pad, :] = ext_ref[tt:tt + pad, :]

    mu = jnp.mean(y, axis=-1, keepdims=True)
    yc = y - mu
    var = jnp.mean(yc * yc, axis=-1, keepdims=True)
    z = (yc * lax.rsqrt(var + eps)) * g_ref[...] + b_ref[...]
    o_ref[...] = (z * jax.nn.sigmoid(z)).astype(o_ref.dtype)


def conformer(proj, conv_w, conv_b, ln_g, ln_b, bsz, t, *, tt=256, eps=1e-5):
    m = proj.shape[0]
    nt = t // tt
    w = GROUP_WIDTH
    row = lambda b, i: b * nt + i
    vec = lambda: pl.BlockSpec((1, w), lambda b, i: (0, 0))
    vmem = (2 * (2 * tt * w * 4 + tt * w * 2) + (8 * tt + SUBLANES * (tt + CONF_PAD)) * w * 4
            + VMEM_SLACK)
    return pl.pallas_call(
        functools.partial(_conformer_kernel, tt=tt, eps=eps),
        grid=(bsz, nt),
        in_specs=[pl.BlockSpec((tt, w), lambda b, i: (row(b, i), OFF_CU // w)),
                  pl.BlockSpec((tt, w), lambda b, i: (row(b, i), OFF_CU // w + 1)),
                  pl.BlockSpec((CONV_KERNEL, w), lambda b, i: (0, 0)),
                  vec(), vec(), vec()],
        out_specs=pl.BlockSpec((tt, w), lambda b, i: (row(b, i), 0)),
        out_shape=jax.ShapeDtypeStruct((m, w), BF16),
        scratch_shapes=[pltpu.VMEM((tt + CONF_PAD, w), F32),
                        pltpu.VMEM((SUBLANES - 1, tt + CONF_PAD, w), F32)],
        compiler_params=_params(("parallel", "arbitrary"), vmem),
        name="conformer",
    )(proj, proj, conv_w, conv_b.reshape(1, w), ln_g.reshape(1, w), ln_b.reshape(1, w))


def kernel(x, norm_mix_g, w_in, gla_w_gate2, gla_b_gate, gla_norm_g, lru_conv_w, lru_conv_b, lru_w_a, lru_b_a, lru_w_x, lru_b_x, lru_lambda, conf_conv_w, conf_conv_b, conf_ln_g, conf_ln_b, w_out, norm_ffn_g, ffn_w_gate, ffn_w_up, ffn_w_down, final_norm_g):
    bsz, t, d = x.shape
    depth = w_in.shape[0]
    xf = x.reshape(bsz * t, d)
    w_in_t = jnp.swapaxes(w_in, 1, 2)
    for l in range(depth):
        rank0 = N_PROJ_GLA
        w2 = jnp.pad(gla_w_gate2[l], ((0, LANES - GLA_GATE_RANK), (0, 0))).astype(BF16)
        wax = jnp.concatenate([lru_w_a[l], lru_w_x[l]], axis=-1).astype(BF16)

        h = rmsnorm(xf, norm_mix_g[l], BF16)
        proj_gla = ws_matmul([h], [w_in_t], l, out_dtype=F32, tm=512, tn=1024, ncols=N_PROJ_GLA,
                             transposed=True, name="in_proj_gla")
        proj = ws_matmul([h], [w_in_t], l, out_dtype=F32, tm=512, tn=1024, col0=rank0,
                         ncols=N_PROJ_REST, shift=GLA_GATE_RANK, transposed=True, name="in_proj_rest")
        glr = ws_matmul([h], [w_in_t], l, out_dtype=F32, tm=1024, tn=LANES, col0=rank0, ncols=LANES,
                        transposed=True, name="in_proj_rank")
        o_a = gla(proj_gla, glr, w2, gla_b_gate[l], gla_norm_g[l], bsz, t)
        o_b = stick_breaking(proj, bsz, t)
        o_c = rglru(proj, lru_conv_w[l], lru_conv_b[l], wax, lru_b_a[l], lru_b_x[l], lru_lambda[l], bsz, t)
        o_d = conformer(proj, conf_conv_w[l], conf_conv_b[l], conf_ln_g[l], conf_ln_b[l], bsz, t)
        xf = ws_matmul([o_a, o_b, o_c, o_d], [w_out], l, xf, out_dtype=F32, tm=512, tn=1024,
                       name="out_proj")

        h = rmsnorm(xf, norm_ffn_g[l], BF16)
        act, w_down = ws_matmul([h], [ffn_w_gate, ffn_w_up], l, out_dtype=BF16, tm=1024, tn=512,
                                swiglu=True, side=ffn_w_down, name="ffn_gate_up")
        xf = matmul([act], w_down, xf, out_dtype=F32, tm=512, tn=512)
    return rmsnorm(xf, final_norm_g, F32).reshape(bsz, t, d)
```

```python
import functools

import jax
import jax.numpy as jnp
from jax import lax
from jax.experimental import pallas as pl
from jax.experimental.pallas import tpu as pltpu

F32 = jnp.float32
BF16 = jnp.bfloat16

GROUP_WIDTH = 1024
GLA_HEADS = 4
GLA_DK = 128
GLA_DV = 256
GLA_GATE_RANK = 16
GLA_GATE_TAU = 16.0
GLA_DIAG = 8
SB_HEADS = 8
SB_HEAD_DIM = 128
LRU_BLOCKS = 8
LRU_BLOCK_DIM = 128
LRU_CONV = 4
LRU_C = 8.0
CONV_KERNEL = 31
LANES = 128
SUBLANES = 8

OFF_GQ, OFF_GK, OFF_GV, OFF_GG = 0, 512, 1024, 2048
N_PROJ_GLA = 3072
OFF_SQ, OFF_SK, OFF_SV = 0, 1024, 2048
OFF_LX, OFF_LG, OFF_CU = 3072, 4096, 5120
N_PROJ_REST = 7168
SB_DEAD_LOG = -105.0

VMEM_SLACK = 6 << 20


def _params(sem, vmem_bytes):
    return pltpu.CompilerParams(dimension_semantics=sem, vmem_limit_bytes=int(vmem_bytes))


def _rmsnorm_kernel(x_ref, g_ref, o_ref, *, eps):
    x = x_ref[...]
    ms = jnp.mean(x * x, axis=-1, keepdims=True)
    o_ref[...] = ((x * lax.rsqrt(ms + eps)) * g_ref[...]).astype(o_ref.dtype)


def rmsnorm(x, g, out_dtype, tm=256, eps=1e-6):
    m, d = x.shape
    return pl.pallas_call(
        functools.partial(_rmsnorm_kernel, eps=eps),
        grid=(m // tm,),
        in_specs=[pl.BlockSpec((tm, d), lambda i: (i, 0)),
                  pl.BlockSpec((1, d), lambda i: (0, 0))],
        out_specs=pl.BlockSpec((tm, d), lambda i: (i, 0)),
        out_shape=jax.ShapeDtypeStruct((m, d), out_dtype),
        compiler_params=_params(("parallel",), 4 * tm * d * 4 + VMEM_SLACK),
        name="rmsnorm",
    )(x, g.reshape(1, d))


def _res_matmul_kernel(*refs, has_emit):
    a_ref, w_ref, res_ref = refs[:3]
    gain_ref = refs[3] if has_emit else None
    o_ref = refs[3 + has_emit]
    out = jnp.dot(a_ref[...], w_ref[...], preferred_element_type=F32) + res_ref[...]
    o_ref[...] = out
    if has_emit:
        _emit_norm_input(out, gain_ref, refs[5], refs[6])


def res_matmul(a, w, res, *, tm, tn, next_gain=None):
    m, k = a.shape
    n = w.shape[1]
    assert m % tm == 0 and n % tn == 0
    blk = lambda: pl.BlockSpec((tm, tn), lambda i, j: (i, j))
    in_specs = [pl.BlockSpec((tm, k), lambda i, j: (i, 0)),
                pl.BlockSpec((k, tn), lambda i, j: (0, j)), blk()]
    args = [a, w, res]
    out_specs = [blk()]
    out_shape = [jax.ShapeDtypeStruct((m, n), F32)]
    if next_gain is not None:
        in_specs.append(pl.BlockSpec((1, tn), lambda i, j: (0, j)))
        args.append(next_gain.reshape(1, n))
        out_specs += [blk(), pl.BlockSpec((tm, LANES), lambda i, j: (i, j))]
        out_shape += [jax.ShapeDtypeStruct((m, n), BF16),
                      jax.ShapeDtypeStruct((m, (n // tn) * LANES), F32)]
    vmem = 2 * (tm * k * 2 + k * tn * 2 + tm * tn * 10 + tm * LANES * 4) + 2 * tm * tn * 4 + VMEM_SLACK
    outs = pl.pallas_call(
        functools.partial(_res_matmul_kernel, has_emit=next_gain is not None),
        grid=(m // tm, n // tn),
        in_specs=in_specs,
        out_specs=out_specs,
        out_shape=out_shape,
        compiler_params=_params(("parallel", "arbitrary"), vmem),
        name="down_proj",
    )(*args)
    return outs[0] if len(outs) == 1 else tuple(outs)


def _row_scale(ssq_ref, inv_d, eps):
    parts = ssq_ref.shape[1] // LANES
    ssq = ssq_ref[:, 0:1]
    for p in range(1, parts):
        ssq = ssq + ssq_ref[:, p * LANES:p * LANES + 1]
    return lax.rsqrt(ssq * inv_d + eps)


def _emit_norm_input(out, gain_ref, xg_ref, ssq_ref):
    xg_ref[...] = (out * gain_ref[...]).astype(xg_ref.dtype)
    ssq_ref[...] = jnp.broadcast_to(jnp.sum(out * out, axis=-1, keepdims=True), ssq_ref.shape)


def _ws_matmul_kernel(*refs, n_a, n_w, ncb, kc, transposed, shift, has_res, has_side, has_scale,
                      has_emit, swiglu, inv_d, eps):
    a_refs = refs[:n_a]
    w_refs = refs[n_a:n_a + n_w]
    pos = n_a + n_w
    x_refs = refs[pos:pos + n_w] if shift else (None,) * n_w
    pos += n_w if shift else 0
    res_ref = refs[pos] if has_res else None
    pos += has_res
    side_ref = refs[pos] if has_side else None
    pos += has_side
    scale_ref = refs[pos] if has_scale else None
    pos += has_scale
    gain_ref = refs[pos] if has_emit else None
    pos += has_emit
    o_ref = refs[pos]
    pos += 1
    if has_side:
        refs[pos][...] = side_ref[...].astype(BF16)
        pos += 1
    xg_ref, ssq_ref = (refs[pos], refs[pos + 1]) if has_emit else (None, None)
    pos += 2 * has_emit
    wb_slots = (refs[pos:pos + n_w], refs[pos + n_w:pos + 2 * n_w])
    j = pl.program_id(0)
    i = pl.program_id(1)

    def convert_chunk(slot):
        rows = pl.ds(pl.multiple_of(i * kc, kc), kc)
        for w_ref, x_ref, wb_ref in zip(w_refs, x_refs, wb_slots[slot]):
            if not transposed:
                w = w_ref[...]
            elif shift:
                w = jnp.concatenate([w_ref[shift:, :], x_ref[...]], axis=0).T
            else:
                w = w_ref[...].T
            wb_ref[rows, :] = w.astype(BF16)

    def multiply(slot):
        ka = a_refs[0].shape[1]
        outs = []
        for wb_ref in wb_slots[slot]:
            acc = None
            for g, a_ref in enumerate(a_refs):
                d = jnp.dot(a_ref[...], wb_ref[g * ka:(g + 1) * ka, :], preferred_element_type=F32)
                acc = d if acc is None else acc + d
            outs.append(acc)
        if has_scale:
            r = _row_scale(scale_ref, inv_d, eps)
            outs = [acc * r for acc in outs]
        if swiglu:
            gate, up = outs
            out = (gate * jax.nn.sigmoid(gate)) * up
        else:
            out = outs[0]
        if has_res:
            out = out + res_ref[...]
        o_ref[...] = out.astype(o_ref.dtype)
        if has_emit:
            _emit_norm_input(out, gain_ref, xg_ref, ssq_ref)

    @pl.when(j == 0)
    def _():
        convert_chunk(0)

    for parity in (0, 1):
        @pl.when((j > 0) & (j % 2 == parity))
        def _():
            convert_chunk(parity)
            multiply(1 - parity)


def ws_matmul(a_list, w_list, layer, res=None, *, out_dtype, tm, tn, col0=0, ncols=None, shift=0,
              transposed=False, swiglu=False, side=None, row_ssq=None, next_gain=None, eps=1e-6,
              name="ws_matmul"):
    n_a, n_w = len(a_list), len(w_list)
    m, ka = a_list[0].shape
    if transposed:
        _, n_total, k_total = w_list[0].shape
    else:
        _, k_total, n_total = w_list[0].shape
    ncols = n_total if ncols is None else ncols
    nm = m // tm
    kc = k_total // nm
    assert ka * n_a == k_total and m % tm == 0 and col0 % tn == 0 and n_w == (2 if swiglu else 1)
    assert kc * nm == k_total and kc % LANES == 0
    assert shift % SUBLANES == 0 and tn % shift == 0 if shift else True
    assert transposed or not shift
    assert ncols % tn == 0 or (col0 + ncols == n_total and not shift)
    ncb = pl.cdiv(ncols, tn)
    cb = col0 // tn

    row_blk = lambda j, i: jnp.where(j > 0, i, 0)
    out_col = lambda j: jnp.maximum(j - 1, 0)
    w_col = lambda j: cb + jnp.minimum(j, ncb - 1)
    w_chunk = lambda j, i: jnp.where(j < ncb, i, nm - 1)

    in_specs = [pl.BlockSpec((tm, ka), lambda j, i: (row_blk(j, i), 0)) for _ in a_list]
    if transposed:
        in_specs += [pl.BlockSpec((None, tn, kc), lambda j, i: (layer, w_col(j), w_chunk(j, i)))
                     for _ in w_list]
    else:
        in_specs += [pl.BlockSpec((None, kc, tn), lambda j, i: (layer, w_chunk(j, i), w_col(j)))
                     for _ in w_list]
    args = list(a_list) + list(w_list)
    if shift:
        per = tn // shift
        in_specs += [pl.BlockSpec((None, shift, kc),
                                  lambda j, i: (layer, (w_col(j) + 1) * per, w_chunk(j, i)))
                     for _ in w_list]
        args += list(w_list)
    if res is not None:
        in_specs.append(pl.BlockSpec((tm, tn), lambda j, i: (row_blk(j, i), out_col(j))))
        args.append(res)
    out_specs = [pl.BlockSpec((tm, tn), lambda j, i: (row_blk(j, i), out_col(j)))]
    out_shape = [jax.ShapeDtypeStruct((m, ncols), out_dtype)]
    side_bytes = 0
    if side is not None:
        _, srows, scols = side.shape
        bf16_rows = 2 * SUBLANES
        slab = next(s for s in range(bf16_rows, srows + 1, bf16_rows)
                    if srows % s == 0 and srows // s <= (ncb + 1) * nm)
        last = srows // slab - 1
        in_specs.append(pl.BlockSpec((None, slab, scols),
                                     lambda j, i: (layer, jnp.minimum(j * nm + i, last), 0)))
        args.append(side)
        out_specs.append(pl.BlockSpec((slab, scols), lambda j, i: (jnp.minimum(j * nm + i, last), 0)))
        out_shape.append(jax.ShapeDtypeStruct((srows, scols), BF16))
        side_bytes = slab * scols * 6
    extra_bytes = 0
    if row_ssq is not None:
        in_specs.append(pl.BlockSpec((tm, row_ssq.shape[1]), lambda j, i: (row_blk(j, i), 0)))
        args.append(row_ssq)
        extra_bytes += tm * row_ssq.shape[1] * 4
    if next_gain is not None:
        assert ncols % tn == 0
        in_specs.append(pl.BlockSpec((1, tn), lambda j, i: (0, out_col(j))))
        args.append(next_gain.reshape(1, ncols))
        out_specs.append(pl.BlockSpec((tm, tn), lambda j, i: (row_blk(j, i), out_col(j))))
        out_shape.append(jax.ShapeDtypeStruct((m, ncols), BF16))
        out_specs.append(pl.BlockSpec((tm, LANES), lambda j, i: (row_blk(j, i), out_col(j))))
        out_shape.append(jax.ShapeDtypeStruct((m, ncb * LANES), F32))
        extra_bytes += tm * tn * 2 + tm * LANES * 4
    osz = jnp.dtype(out_dtype).itemsize
    vmem = (2 * (n_a * tm * ka * 2 + n_w * kc * (tn + shift) * 4 + tm * tn * osz
                 + (tm * tn * 4 if res is not None else 0) + side_bytes + extra_bytes)
            + 2 * n_w * k_total * tn * 2 + (n_w + 1) * tm * tn * 4 + VMEM_SLACK)
    outs = pl.pallas_call(
        functools.partial(_ws_matmul_kernel, n_a=n_a, n_w=n_w, ncb=ncb, kc=kc, transposed=transposed,
                          shift=shift, has_res=res is not None, has_side=side is not None,
                          has_scale=row_ssq is not None, has_emit=next_gain is not None,
                          swiglu=swiglu, inv_d=1.0 / k_total, eps=eps),
        grid=(ncb + 1, nm),
        in_specs=in_specs,
        out_specs=out_specs,
        out_shape=out_shape,
        scratch_shapes=[pltpu.VMEM((k_total, tn), BF16) for _ in range(2 * n_w)],
        compiler_params=_params(("arbitrary", "arbitrary"), vmem),
        name=name,
    )(*args)
    return outs[0] if len(outs) == 1 else tuple(outs)


def _softplus_parts(z):
    l = jnp.log(1.0 + jnp.exp(-jnp.abs(z)))
    return jnp.minimum(z, 0.0) - l, -jnp.maximum(z, 0.0) - l


def _split_bf16(x):
    hi = x.astype(BF16)
    lo = (x - hi.astype(F32)).astype(BF16)
    return hi, lo


def _dot_nt(a, b):
    return lax.dot_general(a, b, (((1,), (1,)), ((), ())), preferred_element_type=F32)


def _dot_tn(a, b):
    return lax.dot_general(a, b, (((0,), (0,)), ((), ())), preferred_element_type=F32)


def _iota2(shape, dim):
    return lax.broadcasted_iota(jnp.int32, shape, dim)


def _sb_kernel(q_ref, k_ref, v_ref, o_ref, *, tb, scale):
    i = pl.program_id(2)
    hd = SB_HEAD_DIM
    nh = q_ref.shape[1] // hd
    row = _iota2((tb, tb), 0)
    col = _iota2((tb, tb), 1)
    tri = jnp.where(row > col, 1.0, 0.0).astype(BF16)
    before = col < row
    qs = [q_ref[:, h * hd:(h + 1) * hd].astype(BF16) for h in range(nh)]

    def block(h, ks, carry, acc, diagonal):
        kb = k_ref[pl.ds(ks, tb), h * hd:(h + 1) * hd].astype(BF16)
        vb = v_ref[pl.ds(ks, tb), h * hd:(h + 1) * hd].astype(BF16)
        z = _dot_nt(qs[h], kb) * scale
        log_beta, log_rest = _softplus_parts(z)
        if diagonal:
            log_rest = jnp.where(before, log_rest, 0.0)
        hi, lo = _split_bf16(log_rest)
        tail = (jnp.dot(hi, tri, preferred_element_type=F32)
                + jnp.dot(lo, tri, preferred_element_type=F32))
        w = jnp.exp(log_beta + tail + carry)
        if diagonal:
            w = jnp.where(before, w, 0.0)
        acc = acc + jnp.dot(w.astype(BF16), vb, preferred_element_type=F32)
        carry = carry + jnp.sum(log_rest, axis=-1, keepdims=True)
        return carry, acc

    def alive(carries):
        top = carries[0]
        for c in carries[1:]:
            top = jnp.maximum(top, c)
        return (jnp.max(top) >= SB_DEAD_LOG).astype(jnp.int32)

    k0 = pl.multiple_of(i * tb, tb)
    state = [block(h, k0, jnp.zeros((tb, 1), F32), jnp.zeros((tb, hd), F32), True)
             for h in range(nh)]
    carries = tuple(s[0] for s in state)
    accs = tuple(s[1] for s in state)

    def cond(st):
        return (st[0] < i) & (st[1] > 0)

    def body(st):
        jb, _, carries, accs = st
        ks = pl.multiple_of((i - 1 - jb) * tb, tb)
        new = [block(h, ks, carries[h], accs[h], False) for h in range(nh)]
        carries = tuple(s[0] for s in new)
        return jb + 1, alive(carries), carries, tuple(s[1] for s in new)

    _, _, _, accs = lax.while_loop(cond, body, (jnp.int32(0), alive(carries), carries, accs))
    for h in range(nh):
        o_ref[:, h * hd:(h + 1) * hd] = accs[h].astype(o_ref.dtype)


def stick_breaking(proj, bsz, t, *, tb=256, heads_per_step=2):
    m = proj.shape[0]
    nq = t // tb
    wd = heads_per_step * SB_HEAD_DIM
    cq, ck, cv = OFF_SQ // wd, OFF_SK // wd, OFF_SV // wd
    vmem = (2 * (tb * wd * 4 + 2 * t * wd * 4 + tb * wd * 2)
            + heads_per_step * 12 * tb * tb * 4 + VMEM_SLACK)
    return pl.pallas_call(
        functools.partial(_sb_kernel, tb=tb, scale=SB_HEAD_DIM ** -0.5),
        grid=(bsz, SB_HEADS // heads_per_step, nq),
        in_specs=[pl.BlockSpec((tb, wd), lambda b, h, i: (b * nq + i, cq + h)),
                  pl.BlockSpec((t, wd), lambda b, h, i: (b, ck + h)),
                  pl.BlockSpec((t, wd), lambda b, h, i: (b, cv + h))],
        out_specs=pl.BlockSpec((tb, wd), lambda b, h, i: (b * nq + i, h)),
        out_shape=jax.ShapeDtypeStruct((m, GROUP_WIDTH), BF16),
        compiler_params=_params(("parallel", "parallel", "arbitrary"), vmem),
        name="stick_breaking",
    )(proj, proj, proj)


def _gla_kernel(gq_ref, gk_ref, gv_ref, gg_ref, glr_ref, w2_ref, bg_ref, ng_ref, o_ref, st_ref,
                *, tt, eps):
    @pl.when(pl.program_id(1) == 0)
    def _():
        st_ref[...] = jnp.zeros_like(st_ref)

    row = _iota2((tt, tt), 0)
    col = _iota2((tt, tt), 1)
    cmr = col - row

    gate = jnp.dot(glr_ref[...].astype(BF16), w2_ref[...], preferred_element_type=F32) + bg_ref[...]
    log_alpha = _softplus_parts(gate)[0] * (1.0 / GLA_GATE_TAU)
    la_hi, la_lo = _split_bf16(log_alpha)

    def rowsum(sel):
        s = sel.astype(BF16)
        return (jnp.dot(s, la_hi, preferred_element_type=F32)
                + jnp.dot(s, la_lo, preferred_element_type=F32))

    cum = rowsum(jnp.where(col <= row, 1.0, 0.0))
    last = cum[tt - 1:tt, :]

    levels = []
    s = tt // 2
    while s >= GLA_DIAG:
        start_r = (row & ~(2 * s - 1)) + s
        is_r = (row & s) != 0
        sel = jnp.where(is_r,
                        jnp.where((col >= start_r) & (col <= row), 1.0, 0.0),
                        jnp.where((col > row) & (col < start_r), 1.0, 0.0))
        same_pair = (row & ~(2 * s - 1)) == (col & ~(2 * s - 1))
        levels.append((s, jnp.exp(rowsum(sel)), same_pair))
        s //= 2

    q_all = gq_ref[...] * (GLA_DK ** -0.5)
    k_all = gk_ref[...]
    v_all = gv_ref[...]
    rowl = _iota2((tt, GLA_DK), 0)
    dv = GLA_DV
    for h in range(GLA_HEADS):
        ksl = slice(h * GLA_DK, (h + 1) * GLA_DK)
        vsl = slice(h * dv, (h + 1) * dv)
        q, k, c = q_all[:, ksl], k_all[:, ksl], cum[:, ksl]
        v_bf = v_all[:, vsl].astype(BF16)
        st = st_ref[h]

        o = _dot_nt((q * jnp.exp(c)).astype(BF16), st.astype(BF16))

        scores = jnp.zeros((tt, tt), F32)
        for s, decay, same_pair in levels:
            d = decay[:, ksl]
            is_r = (rowl & s) != 0
            qh = jnp.where(is_r, q * d, 0.0).astype(BF16)
            kh = jnp.where(is_r, 0.0, k * d).astype(BF16)
            sc = _dot_nt(qh, kh)
            scores = scores + (sc if 2 * s == tt else jnp.where(same_pair, sc, 0.0))

        for dd in range(GLA_DIAG):
            k_sh = k if dd == 0 else pltpu.roll(k, dd, 0)
            c_sh = c if dd == 0 else pltpu.roll(c, dd, 0)
            e = jnp.exp(jnp.minimum(c - c_sh, 0.0))
            sd = jnp.sum(q * k_sh * e, axis=-1, keepdims=True)
            ok = (cmr == -dd) & ((row & (GLA_DIAG - 1)) >= dd)
            scores = scores + jnp.where(ok, sd, 0.0)

        o = o + jnp.dot(scores.astype(BF16), v_bf, preferred_element_type=F32)

        lh = last[:, ksl]
        k_dec = (k * jnp.exp(lh - c)).astype(BF16)
        st_ref[h] = st * jnp.exp(lh) + _dot_tn(v_bf, k_dec)

        ms = jnp.mean(o * o, axis=-1, keepdims=True)
        y = (o * lax.rsqrt(ms + eps)) * ng_ref[...]
        g_out = gg_ref[:, vsl]
        o_ref[:, vsl] = (y * (g_out * jax.nn.sigmoid(g_out))).astype(o_ref.dtype)


def gla(proj, glr, w2, b_gate, norm_g, bsz, t, *, tt=128, eps=1e-6):
    m = proj.shape[0]
    nt = t // tt
    hk = GLA_HEADS * GLA_DK
    row = lambda b, i: b * nt + i
    vmem = (2 * (2 * tt * hk * 4 + 2 * tt * GROUP_WIDTH * 4 + tt * LANES * 4 + tt * GROUP_WIDTH * 2)
            + GLA_HEADS * GLA_DV * GLA_DK * 4 + 24 * tt * hk * 4 + VMEM_SLACK)
    return pl.pallas_call(
        functools.partial(_gla_kernel, tt=tt, eps=eps),
        grid=(bsz, nt),
        in_specs=[pl.BlockSpec((tt, hk), lambda b, i: (row(b, i), OFF_GQ // hk)),
                  pl.BlockSpec((tt, hk), lambda b, i: (row(b, i), OFF_GK // hk)),
                  pl.BlockSpec((tt, GROUP_WIDTH), lambda b, i: (row(b, i), OFF_GV // GROUP_WIDTH)),
                  pl.BlockSpec((tt, GROUP_WIDTH), lambda b, i: (row(b, i), OFF_GG // GROUP_WIDTH)),
                  pl.BlockSpec((tt, LANES), lambda b, i: (row(b, i), 0)),
                  pl.BlockSpec((LANES, hk), lambda b, i: (0, 0)),
                  pl.BlockSpec((1, hk), lambda b, i: (0, 0)),
                  pl.BlockSpec((1, GLA_DV), lambda b, i: (0, 0))],
        out_specs=pl.BlockSpec((tt, GROUP_WIDTH), lambda b, i: (row(b, i), 0)),
        out_shape=jax.ShapeDtypeStruct((m, GROUP_WIDTH), BF16),
        scratch_shapes=[pltpu.VMEM((GLA_HEADS, GLA_DV, GLA_DK), F32)],
        compiler_params=_params(("parallel", "arbitrary"), vmem),
        name="gla",
    )(proj, proj, proj, proj, glr, w2, b_gate.reshape(1, hk), norm_g.reshape(1, GLA_DV))


def _lru_kernel(lx_ref, lg_ref, cw_ref, cb_ref, wax_ref, ba_ref, bx_ref, lam_ref, o_ref,
                ext_ref, h_ref, *, tt):
    pad = SUBLANES

    @pl.when(pl.program_id(1) == 0)
    def _():
        ext_ref[0:pad, :] = jnp.zeros((pad, ext_ref.shape[1]), F32)
        h_ref[...] = jnp.zeros_like(h_ref)

    ext_ref[pad:pad + tt, :] = lx_ref[...]
    xc = cb_ref[...] + cw_ref[0:1, :] * ext_ref[pl.ds(pad - LRU_CONV + 1, tt), :]
    for kk in range(1, LRU_CONV):
        xc = xc + cw_ref[kk:kk + 1, :] * ext_ref[pl.ds(pad - LRU_CONV + 1 + kk, tt), :]
    ext_ref[0:pad, :] = ext_ref[tt:tt + pad, :]

    bd = LRU_BLOCK_DIM
    r_parts, i_parts = [], []
    for n in range(LRU_BLOCKS):
        ri = jnp.dot(xc[:, n * bd:(n + 1) * bd].astype(BF16), wax_ref[n], preferred_element_type=F32)
        r_parts.append(ri[:, :bd])
        i_parts.append(ri[:, bd:])
    r = jax.nn.sigmoid(jnp.concatenate(r_parts, axis=1) + ba_ref[...])
    ig = jax.nn.sigmoid(jnp.concatenate(i_parts, axis=1) + bx_ref[...])

    lam = lam_ref[...]
    softplus_neg_lam = jnp.maximum(-lam, 0.0) + jnp.log1p(jnp.exp(-jnp.abs(lam)))
    log_a = (-LRU_C) * r * softplus_neg_lam
    a = jnp.exp(log_a)
    u = jnp.sqrt(-jnp.tanh(log_a) * (a * a + 1.0)) * (ig * xc)

    rowi = _iota2(a.shape, 0)
    sft = 1
    while sft < tt:
        keep = rowi >= sft
        a_prev = jnp.where(keep, pltpu.roll(a, sft, 0), 1.0)
        u_prev = jnp.where(keep, pltpu.roll(u, sft, 0), 0.0)
        u = a * u_prev + u
        a = a * a_prev
        sft *= 2
    h = u + a * h_ref[0:1, :]
    h_ref[0:1, :] = h[tt - 1:tt, :]

    lg = lg_ref[...]
    gelu = 0.5 * lg * (1.0 + jnp.tanh(0.7978845608028654 * (lg + 0.044715 * (lg * lg * lg))))
    o_ref[...] = (h * gelu).astype(o_ref.dtype)


def rglru(proj, conv_w, conv_b, wax, b_a, b_x, lam, bsz, t, *, tt=256):
    m = proj.shape[0]
    nt = t // tt
    w = GROUP_WIDTH
    row = lambda b, i: b * nt + i
    vec = lambda: pl.BlockSpec((1, w), lambda b, i: (0, 0))
    vmem = 2 * (2 * tt * w * 4 + tt * w * 2) + 24 * tt * w * 4 + VMEM_SLACK
    return pl.pallas_call(
        functools.partial(_lru_kernel, tt=tt),
        grid=(bsz, nt),
        in_specs=[pl.BlockSpec((tt, w), lambda b, i: (row(b, i), OFF_LX // w)),
                  pl.BlockSpec((tt, w), lambda b, i: (row(b, i), OFF_LG // w)),
                  pl.BlockSpec((LRU_CONV, w), lambda b, i: (0, 0)),
                  vec(),
                  pl.BlockSpec((LRU_BLOCKS, LRU_BLOCK_DIM, 2 * LRU_BLOCK_DIM), lambda b, i: (0, 0, 0)),
                  vec(), vec(), vec()],
        out_specs=pl.BlockSpec((tt, w), lambda b, i: (row(b, i), 0)),
        out_shape=jax.ShapeDtypeStruct((m, w), BF16),
        scratch_shapes=[pltpu.VMEM((tt + SUBLANES, w), F32), pltpu.VMEM((SUBLANES, w), F32)],
        compiler_params=_params(("parallel", "arbitrary"), vmem),
        name="rglru",
    )(proj, proj, conv_w, conv_b.reshape(1, w), wax, b_a.reshape(1, w), b_x.reshape(1, w),
      lam.reshape(1, w))


CONF_PAD = 32


def _conformer_kernel(val_ref, gte_ref, cw_ref, cb_ref, g_ref, b_ref, o_ref, ext_ref, sh_ref,
                      *, tt, eps):
    pad = CONF_PAD
    rows = tt + pad

    @pl.when(pl.program_id(1) == 0)
    def _():
        ext_ref[0:pad, :] = jnp.zeros((pad, ext_ref.shape[1]), F32)

    ext_ref[pad:pad + tt, :] = val_ref[...] * jax.nn.sigmoid(gte_ref[...])
    ext = ext_ref[...]
    for b in range(1, SUBLANES):
        sh_ref[b - 1] = pltpu.roll(ext, rows - b, 0)
    base = pad - CONV_KERNEL + 1
    y = cb_ref[...]
    for kk in range(CONV_KERNEL):
        a, b = divmod(base + kk, SUBLANES)
        src = ext_ref if b == 0 else sh_ref.at[b - 1]
        y = y + cw_ref[kk:kk + 1, :] * src[pl.ds(a * SUBLANES, tt), :]
    ext_ref[0:pad, :] = ext_ref[tt:tt + pad, :]

    mu = jnp.mean(y, axis=-1, keepdims=True)
    yc = y - mu
    var = jnp.mean(yc * yc, axis=-1, keepdims=True)
    z = (yc * lax.rsqrt(var + eps)) * g_ref[...] + b_ref[...]
    o_ref[...] = (z * jax.nn.sigmoid(z)).astype(o_ref.dtype)


def conformer(proj, conv_w, conv_b, ln_g, ln_b, bsz, t, *, tt=256, eps=1e-5):
    m = proj.shape[0]
    nt = t // tt
    w = GROUP_WIDTH
    row = lambda b, i: b * nt + i
    vec = lambda: pl.BlockSpec((1, w), lambda b, i: (0, 0))
    vmem = (2 * (2 * tt * w * 4 + tt * w * 2) + (8 * tt + SUBLANES * (tt + CONF_PAD)) * w * 4
            + VMEM_SLACK)
    return pl.pallas_call(
        functools.partial(_conformer_kernel, tt=tt, eps=eps),
        grid=(bsz, nt),
        in_specs=[pl.BlockSpec((tt, w), lambda b, i: (row(b, i), OFF_CU // w)),
                  pl.BlockSpec((tt, w), lambda b, i: (row(b, i), OFF_CU // w + 1)),
                  pl.BlockSpec((CONV_KERNEL, w), lambda b, i: (0, 0)),
                  vec(), vec(), vec()],
        out_specs=pl.BlockSpec((tt, w), lambda b, i: (row(b, i), 0)),
        out_shape=jax.ShapeDtypeStruct((m, w), BF16),
        scratch_shapes=[pltpu.VMEM((tt + CONF_PAD, w), F32),
                        pltpu.VMEM((SUBLANES - 1, tt + CONF_PAD, w), F32)],
        compiler_params=_params(("parallel", "arbitrary"), vmem),
        name="conformer",
    )(proj, proj, conv_w, conv_b.reshape(1, w), ln_g.reshape(1, w), ln_b.reshape(1, w))


def kernel(x, norm_mix_g, w_in, gla_w_gate2, gla_b_gate, gla_norm_g, lru_conv_w, lru_conv_b, lru_w_a, lru_b_a, lru_w_x, lru_b_x, lru_lambda, conf_conv_w, conf_conv_b, conf_ln_g, conf_ln_b, w_out, norm_ffn_g, ffn_w_gate, ffn_w_up, ffn_w_down, final_norm_g):
    bsz, t, d = x.shape
    depth = w_in.shape[0]
    xf = x.reshape(bsz * t, d)
    w_in_t = jnp.swapaxes(w_in, 1, 2)
    h = rmsnorm(xf, norm_mix_g[0], BF16)
    ssq = None
    for l in range(depth):
        rank0 = N_PROJ_GLA
        w2 = jnp.pad(gla_w_gate2[l], ((0, LANES - GLA_GATE_RANK), (0, 0))).astype(BF16)
        wax = jnp.concatenate([lru_w_a[l], lru_w_x[l]], axis=-1).astype(BF16)

        proj_gla = ws_matmul([h], [w_in_t], l, out_dtype=F32, tm=512, tn=1024, ncols=N_PROJ_GLA,
                             transposed=True, row_ssq=ssq, name="in_proj_gla")
        proj = ws_matmul([h], [w_in_t], l, out_dtype=F32, tm=512, tn=1024, col0=rank0,
                         ncols=N_PROJ_REST, shift=GLA_GATE_RANK, transposed=True, row_ssq=ssq,
                         name="in_proj_rest")
        glr = ws_matmul([h], [w_in_t], l, out_dtype=F32, tm=1024, tn=LANES, col0=rank0, ncols=LANES,
                        transposed=True, row_ssq=ssq, name="in_proj_rank")
        o_a = gla(proj_gla, glr, w2, gla_b_gate[l], gla_norm_g[l], bsz, t)
        o_b = stick_breaking(proj, bsz, t)
        o_c = rglru(proj, lru_conv_w[l], lru_conv_b[l], wax, lru_b_a[l], lru_b_x[l], lru_lambda[l], bsz, t)
        o_d = conformer(proj, conf_conv_w[l], conf_conv_b[l], conf_ln_g[l], conf_ln_b[l], bsz, t)
        xf, h, ssq = ws_matmul([o_a, o_b, o_c, o_d], [w_out], l, xf, out_dtype=F32, tm=512, tn=1024,
                               next_gain=norm_ffn_g[l], name="out_proj")

        act, w_down = ws_matmul([h], [ffn_w_gate, ffn_w_up], l, out_dtype=BF16, tm=1024, tn=512,
                                swiglu=True, side=ffn_w_down, row_ssq=ssq, name="ffn_gate_up")
        if l + 1 < depth:
            xf, h, ssq = res_matmul(act, w_down, xf, tm=512, tn=512, next_gain=norm_mix_g[l + 1])
        else:
            xf = res_matmul(act, w_down, xf, tm=512, tn=512)
    return rmsnorm(xf, final_norm_g, F32).reshape(bsz, t, d)
```

```python
import functools

import jax
import jax.numpy as jnp
from jax import lax
from jax.experimental import pallas as pl
from jax.experimental.pallas import tpu as pltpu

F32 = jnp.float32
BF16 = jnp.bfloat16

GROUP_WIDTH = 1024
GLA_HEADS = 4
GLA_DK = 128
GLA_DV = 256
GLA_GATE_RANK = 16
GLA_GATE_TAU = 16.0
GLA_DIAG = 8
SB_HEADS = 8
SB_HEAD_DIM = 128
LRU_BLOCKS = 8
LRU_BLOCK_DIM = 128
LRU_CONV = 4
LRU_C = 8.0
CONV_KERNEL = 31
LANES = 128
SUBLANES = 8

OFF_GQ, OFF_GK, OFF_GV, OFF_GG = 0, 512, 1024, 2048
N_PROJ_GLA = 3072
OFF_SQ, OFF_SK, OFF_SV = 0, 1024, 2048
OFF_LX, OFF_LG, OFF_CU = 3072, 4096, 5120
N_PROJ_REST = 7168
SB_DEAD_LOG = -105.0

VMEM_SLACK = 6 << 20


def _params(sem, vmem_bytes):
    return pltpu.CompilerParams(dimension_semantics=sem, vmem_limit_bytes=int(vmem_bytes))


def _rmsnorm_kernel(x_ref, g_ref, o_ref, *, eps):
    x = x_ref[...]
    ms = jnp.mean(x * x, axis=-1, keepdims=True)
    o_ref[...] = ((x * lax.rsqrt(ms + eps)) * g_ref[...]).astype(o_ref.dtype)


def rmsnorm(x, g, out_dtype, tm=256, eps=1e-6):
    m, d = x.shape
    return pl.pallas_call(
        functools.partial(_rmsnorm_kernel, eps=eps),
        grid=(m // tm,),
        in_specs=[pl.BlockSpec((tm, d), lambda i: (i, 0)),
                  pl.BlockSpec((1, d), lambda i: (0, 0))],
        out_specs=pl.BlockSpec((tm, d), lambda i: (i, 0)),
        out_shape=jax.ShapeDtypeStruct((m, d), out_dtype),
        compiler_params=_params(("parallel",), 4 * tm * d * 4 + VMEM_SLACK),
        name="rmsnorm",
    )(x, g.reshape(1, d))


def _res_matmul_kernel(*refs, has_emit):
    a_ref, w_ref, res_ref = refs[:3]
    gain_ref = refs[3] if has_emit else None
    o_ref = refs[3 + has_emit]
    out = jnp.dot(a_ref[...], w_ref[...], preferred_element_type=F32) + res_ref[...]
    o_ref[...] = out
    if has_emit:
        _emit_norm_input(out, gain_ref, refs[5], refs[6])


def res_matmul(a, w, res, *, tm, tn, next_gain=None):
    m, k = a.shape
    n = w.shape[1]
    assert m % tm == 0 and n % tn == 0
    blk = lambda: pl.BlockSpec((tm, tn), lambda i, j: (i, j))
    in_specs = [pl.BlockSpec((tm, k), lambda i, j: (i, 0)),
                pl.BlockSpec((k, tn), lambda i, j: (0, j)), blk()]
    args = [a, w, res]
    out_specs = [blk()]
    out_shape = [jax.ShapeDtypeStruct((m, n), F32)]
    if next_gain is not None:
        in_specs.append(pl.BlockSpec((1, tn), lambda i, j: (0, j)))
        args.append(next_gain.reshape(1, n))
        out_specs += [blk(), pl.BlockSpec((tm, LANES), lambda i, j: (i, j))]
        out_shape += [jax.ShapeDtypeStruct((m, n), BF16),
                      jax.ShapeDtypeStruct((m, (n // tn) * LANES), F32)]
    vmem = 2 * (tm * k * 2 + k * tn * 2 + tm * tn * 10 + tm * LANES * 4) + 2 * tm * tn * 4 + VMEM_SLACK
    outs = pl.pallas_call(
        functools.partial(_res_matmul_kernel, has_emit=next_gain is not None),
        grid=(m // tm, n // tn),
        in_specs=in_specs,
        out_specs=out_specs,
        out_shape=out_shape,
        compiler_params=_params(("parallel", "arbitrary"), vmem),
        name="down_proj",
    )(*args)
    return outs[0] if len(outs) == 1 else tuple(outs)


def _row_scale(ssq_ref, inv_d, eps):
    parts = ssq_ref.shape[1] // LANES
    ssq = ssq_ref[:, 0:1]
    for p in range(1, parts):
        ssq = ssq + ssq_ref[:, p * LANES:p * LANES + 1]
    return lax.rsqrt(ssq * inv_d + eps)


def _emit_norm_input(out, gain_ref, xg_ref, ssq_ref):
    xg_ref[...] = (out * gain_ref[...]).astype(xg_ref.dtype)
    ssq_ref[...] = jnp.broadcast_to(jnp.sum(out * out, axis=-1, keepdims=True), ssq_ref.shape)


def _ws_matmul_kernel(*refs, n_a, n_w, ncb, kc, transposed, shift, has_res, has_side, has_scale,
                      has_emit, swiglu, inv_d, eps):
    a_refs = refs[:n_a]
    w_refs = refs[n_a:n_a + n_w]
    pos = n_a + n_w
    x_refs = refs[pos:pos + n_w] if shift else (None,) * n_w
    pos += n_w if shift else 0
    res_ref = refs[pos] if has_res else None
    pos += has_res
    side_ref = refs[pos] if has_side else None
    pos += has_side
    scale_ref = refs[pos] if has_scale else None
    pos += has_scale
    gain_ref = refs[pos] if has_emit else None
    pos += has_emit
    o_ref = refs[pos]
    pos += 1
    if has_side:
        refs[pos][...] = side_ref[...].astype(BF16)
        pos += 1
    xg_ref, ssq_ref = (refs[pos], refs[pos + 1]) if has_emit else (None, None)
    pos += 2 * has_emit
    wb_slots = (refs[pos:pos + n_w], refs[pos + n_w:pos + 2 * n_w])
    j = pl.program_id(0)
    i = pl.program_id(1)

    def convert_chunk(slot):
        rows = pl.ds(pl.multiple_of(i * kc, kc), kc)
        for w_ref, x_ref, wb_ref in zip(w_refs, x_refs, wb_slots[slot]):
            if not transposed:
                w = w_ref[...]
            elif shift:
                w = jnp.concatenate([w_ref[shift:, :], x_ref[...]], axis=0).T
            else:
                w = w_ref[...].T
            wb_ref[rows, :] = w.astype(BF16)

    def multiply(slot):
        ka = a_refs[0].shape[1]
        outs = []
        for wb_ref in wb_slots[slot]:
            acc = None
            for g, a_ref in enumerate(a_refs):
                d = jnp.dot(a_ref[...], wb_ref[g * ka:(g + 1) * ka, :], preferred_element_type=F32)
                acc = d if acc is None else acc + d
            outs.append(acc)
        if has_scale:
            r = _row_scale(scale_ref, inv_d, eps)
            outs = [acc * r for acc in outs]
        if swiglu:
            gate, up = outs
            out = (gate * jax.nn.sigmoid(gate)) * up
        else:
            out = outs[0]
        if has_res:
            out = out + res_ref[...]
        o_ref[...] = out.astype(o_ref.dtype)
        if has_emit:
            _emit_norm_input(out, gain_ref, xg_ref, ssq_ref)

    @pl.when(j == 0)
    def _():
        convert_chunk(0)

    for parity in (0, 1):
        @pl.when((j > 0) & (j % 2 == parity))
        def _():
            convert_chunk(parity)
            multiply(1 - parity)


def ws_matmul(a_list, w_list, layer, res=None, *, out_dtype, tm, tn, col0=0, ncols=None, shift=0,
              transposed=False, swiglu=False, side=None, row_ssq=None, next_gain=None, eps=1e-6,
              name="ws_matmul"):
    n_a, n_w = len(a_list), len(w_list)
    m, ka = a_list[0].shape
    if transposed:
        _, n_total, k_total = w_list[0].shape
    else:
        _, k_total, n_total = w_list[0].shape
    ncols = n_total if ncols is None else ncols
    nm = m // tm
    kc = k_total // nm
    assert ka * n_a == k_total and m % tm == 0 and col0 % tn == 0 and n_w == (2 if swiglu else 1)
    assert kc * nm == k_total and kc % LANES == 0
    assert shift % SUBLANES == 0 and tn % shift == 0 if shift else True
    assert transposed or not shift
    assert ncols % tn == 0 or (col0 + ncols == n_total and not shift)
    ncb = pl.cdiv(ncols, tn)
    cb = col0 // tn

    row_blk = lambda j, i: jnp.where(j > 0, i, 0)
    out_col = lambda j: jnp.maximum(j - 1, 0)
    w_col = lambda j: cb + jnp.minimum(j, ncb - 1)
    w_chunk = lambda j, i: jnp.where(j < ncb, i, nm - 1)

    in_specs = [pl.BlockSpec((tm, ka), lambda j, i: (row_blk(j, i), 0)) for _ in a_list]
    if transposed:
        in_specs += [pl.BlockSpec((None, tn, kc), lambda j, i: (layer, w_col(j), w_chunk(j, i)))
                     for _ in w_list]
    else:
        in_specs += [pl.BlockSpec((None, kc, tn), lambda j, i: (layer, w_chunk(j, i), w_col(j)))
                     for _ in w_list]
    args = list(a_list) + list(w_list)
    if shift:
        per = tn // shift
        in_specs += [pl.BlockSpec((None, shift, kc),
                                  lambda j, i: (layer, (w_col(j) + 1) * per, w_chunk(j, i)))
                     for _ in w_list]
        args += list(w_list)
    if res is not None:
        in_specs.append(pl.BlockSpec((tm, tn), lambda j, i: (row_blk(j, i), out_col(j))))
        args.append(res)
    out_specs = [pl.BlockSpec((tm, tn), lambda j, i: (row_blk(j, i), out_col(j)))]
    out_shape = [jax.ShapeDtypeStruct((m, ncols), out_dtype)]
    side_bytes = 0
    if side is not None:
        _, srows, scols = side.shape
        bf16_rows = 2 * SUBLANES
        slab = next(s for s in range(bf16_rows, srows + 1, bf16_rows)
                    if srows % s == 0 and srows // s <= (ncb + 1) * nm)
        last = srows // slab - 1
        in_specs.append(pl.BlockSpec((None, slab, scols),
                                     lambda j, i: (layer, jnp.minimum(j * nm + i, last), 0)))
        args.append(side)
        out_specs.append(pl.BlockSpec((slab, scols), lambda j, i: (jnp.minimum(j * nm + i, last), 0)))
        out_shape.append(jax.ShapeDtypeStruct((srows, scols), BF16))
        side_bytes = slab * scols * 6
    extra_bytes = 0
    if row_ssq is not None:
        in_specs.append(pl.BlockSpec((tm, row_ssq.shape[1]), lambda j, i: (row_blk(j, i), 0)))
        args.append(row_ssq)
        extra_bytes += tm * row_ssq.shape[1] * 4
    if next_gain is not None:
        assert ncols % tn == 0
        in_specs.append(pl.BlockSpec((1, tn), lambda j, i: (0, out_col(j))))
        args.append(next_gain.reshape(1, ncols))
        out_specs.append(pl.BlockSpec((tm, tn), lambda j, i: (row_blk(j, i), out_col(j))))
        out_shape.append(jax.ShapeDtypeStruct((m, ncols), BF16))
        out_specs.append(pl.BlockSpec((tm, LANES), lambda j, i: (row_blk(j, i), out_col(j))))
        out_shape.append(jax.ShapeDtypeStruct((m, ncb * LANES), F32))
        extra_bytes += tm * tn * 2 + tm * LANES * 4
    osz = jnp.dtype(out_dtype).itemsize
    vmem = (2 * (n_a * tm * ka * 2 + n_w * kc * (tn + shift) * 4 + tm * tn * osz
                 + (tm * tn * 4 if res is not None else 0) + side_bytes + extra_bytes)
            + 2 * n_w * k_total * tn * 2 + (n_w + 1) * tm * tn * 4 + VMEM_SLACK)
    outs = pl.pallas_call(
        functools.partial(_ws_matmul_kernel, n_a=n_a, n_w=n_w, ncb=ncb, kc=kc, transposed=transposed,
                          shift=shift, has_res=res is not None, has_side=side is not None,
                          has_scale=row_ssq is not None, has_emit=next_gain is not None,
                          swiglu=swiglu, inv_d=1.0 / k_total, eps=eps),
        grid=(ncb + 1, nm),
        in_specs=in_specs,
        out_specs=out_specs,
        out_shape=out_shape,
        scratch_shapes=[pltpu.VMEM((k_total, tn), BF16) for _ in range(2 * n_w)],
        compiler_params=_params(("arbitrary", "arbitrary"), vmem),
        name=name,
    )(*args)
    return outs[0] if len(outs) == 1 else tuple(outs)


def _softplus_parts(z):
    l = jnp.log(1.0 + jnp.exp(-jnp.abs(z)))
    return jnp.minimum(z, 0.0) - l, -jnp.maximum(z, 0.0) - l


def _split_bf16(x):
    hi = x.astype(BF16)
    lo = (x - hi.astype(F32)).astype(BF16)
    return hi, lo


def _dot_nt(a, b):
    return lax.dot_general(a, b, (((1,), (1,)), ((), ())), preferred_element_type=F32)


def _dot_tn(a, b):
    return lax.dot_general(a, b, (((0,), (0,)), ((), ())), preferred_element_type=F32)


def _iota2(shape, dim):
    return lax.broadcasted_iota(jnp.int32, shape, dim)


def _sb_kernel(q_ref, k_ref, v_ref, o_ref, *, tb, scale):
    i = pl.program_id(2)
    hd = SB_HEAD_DIM
    nh = q_ref.shape[1] // hd
    row = _iota2((tb, tb), 0)
    col = _iota2((tb, tb), 1)
    tri = jnp.where(row > col, 1.0, 0.0).astype(BF16)
    before = col < row
    qs = [q_ref[:, h * hd:(h + 1) * hd].astype(BF16) for h in range(nh)]

    def blocks(ks, carries, accs, diagonal):
        log_betas, log_rests, pieces = [], [], []
        for h in range(nh):
            kb = k_ref[pl.ds(ks, tb), h * hd:(h + 1) * hd].astype(BF16)
            log_beta, log_rest = _softplus_parts(_dot_nt(qs[h], kb) * scale)
            if diagonal:
                log_rest = jnp.where(before, log_rest, 0.0)
            log_betas.append(log_beta)
            log_rests.append(log_rest)
            pieces += list(_split_bf16(log_rest))
        tails = jnp.dot(jnp.concatenate(pieces, axis=0), tri, preferred_element_type=F32)
        new_carries, new_accs = [], []
        for h in range(nh):
            tail = tails[2 * h * tb:(2 * h + 1) * tb] + tails[(2 * h + 1) * tb:(2 * h + 2) * tb]
            w = jnp.exp(log_betas[h] + tail + carries[h])
            if diagonal:
                w = jnp.where(before, w, 0.0)
            vb = v_ref[pl.ds(ks, tb), h * hd:(h + 1) * hd].astype(BF16)
            new_accs.append(accs[h] + jnp.dot(w.astype(BF16), vb, preferred_element_type=F32))
            new_carries.append(carries[h] + jnp.sum(log_rests[h], axis=-1, keepdims=True))
        return tuple(new_carries), tuple(new_accs)

    def alive(carries):
        top = carries[0]
        for c in carries[1:]:
            top = jnp.maximum(top, c)
        return (jnp.max(top) >= SB_DEAD_LOG).astype(jnp.int32)

    k0 = pl.multiple_of(i * tb, tb)
    carries, accs = blocks(k0, (jnp.zeros((tb, 1), F32),) * nh, (jnp.zeros((tb, hd), F32),) * nh,
                           True)

    def cond(st):
        return (st[0] < i) & (st[1] > 0)

    def body(st):
        jb, _, carries, accs = st
        ks = pl.multiple_of((i - 1 - jb) * tb, tb)
        carries, accs = blocks(ks, carries, accs, False)
        return jb + 1, alive(carries), carries, accs

    _, _, _, accs = lax.while_loop(cond, body, (jnp.int32(0), alive(carries), carries, accs))
    for h in range(nh):
        o_ref[:, h * hd:(h + 1) * hd] = accs[h].astype(o_ref.dtype)


def stick_breaking(proj, bsz, t, *, tb=256, heads_per_step=4):
    m = proj.shape[0]
    nq = t // tb
    wd = heads_per_step * SB_HEAD_DIM
    cq, ck, cv = OFF_SQ // wd, OFF_SK // wd, OFF_SV // wd
    vmem = (2 * (tb * wd * 4 + 2 * t * wd * 4 + tb * wd * 2)
            + heads_per_step * 12 * tb * tb * 4 + VMEM_SLACK)
    return pl.pallas_call(
        functools.partial(_sb_kernel, tb=tb, scale=SB_HEAD_DIM ** -0.5),
        grid=(bsz, SB_HEADS // heads_per_step, nq),
        in_specs=[pl.BlockSpec((tb, wd), lambda b, h, i: (b * nq + i, cq + h)),
                  pl.BlockSpec((t, wd), lambda b, h, i: (b, ck + h)),
                  pl.BlockSpec((t, wd), lambda b, h, i: (b, cv + h))],
        out_specs=pl.BlockSpec((tb, wd), lambda b, h, i: (b * nq + i, h)),
        out_shape=jax.ShapeDtypeStruct((m, GROUP_WIDTH), BF16),
        compiler_params=_params(("parallel", "parallel", "arbitrary"), vmem),
        name="stick_breaking",
    )(proj, proj, proj)


def _gla_kernel(gq_ref, gk_ref, gv_ref, gg_ref, glr_ref, w2_ref, bg_ref, ng_ref, o_ref, st_ref,
                *, tt, eps):
    @pl.when(pl.program_id(1) == 0)
    def _():
        st_ref[...] = jnp.zeros_like(st_ref)

    row = _iota2((tt, tt), 0)
    col = _iota2((tt, tt), 1)
    cmr = col - row

    gate = jnp.dot(glr_ref[...].astype(BF16), w2_ref[...], preferred_element_type=F32) + bg_ref[...]
    log_alpha = _softplus_parts(gate)[0] * (1.0 / GLA_GATE_TAU)
    la_hi, la_lo = _split_bf16(log_alpha)

    def rowsum(sel):
        s = sel.astype(BF16)
        return (jnp.dot(s, la_hi, preferred_element_type=F32)
                + jnp.dot(s, la_lo, preferred_element_type=F32))

    cum = rowsum(jnp.where(col <= row, 1.0, 0.0))
    last = cum[tt - 1:tt, :]

    levels = []
    s = tt // 2
    while s >= GLA_DIAG:
        start_r = (row & ~(2 * s - 1)) + s
        is_r = (row & s) != 0
        sel = jnp.where(is_r,
                        jnp.where((col >= start_r) & (col <= row), 1.0, 0.0),
                        jnp.where((col > row) & (col < start_r), 1.0, 0.0))
        same_pair = (row & ~(2 * s - 1)) == (col & ~(2 * s - 1))
        levels.append((s, jnp.exp(rowsum(sel)), same_pair))
        s //= 2

    q_all = gq_ref[...] * (GLA_DK ** -0.5)
    k_all = gk_ref[...]
    v_all = gv_ref[...]
    rowl = _iota2((tt, GLA_DK), 0)
    dv = GLA_DV
    for h in range(GLA_HEADS):
        ksl = slice(h * GLA_DK, (h + 1) * GLA_DK)
        vsl = slice(h * dv, (h + 1) * dv)
        q, k, c = q_all[:, ksl], k_all[:, ksl], cum[:, ksl]
        v_bf = v_all[:, vsl].astype(BF16)
        st = st_ref[h]

        o = _dot_nt((q * jnp.exp(c)).astype(BF16), st.astype(BF16))

        scores = jnp.zeros((tt, tt), F32)
        for s, decay, same_pair in levels:
            d = decay[:, ksl]
            is_r = (rowl & s) != 0
            qh = jnp.where(is_r, q * d, 0.0).astype(BF16)
            kh = jnp.where(is_r, 0.0, k * d).astype(BF16)
            sc = _dot_nt(qh, kh)
            scores = scores + (sc if 2 * s == tt else jnp.where(same_pair, sc, 0.0))

        for dd in range(GLA_DIAG):
            k_sh = k if dd == 0 else pltpu.roll(k, dd, 0)
            c_sh = c if dd == 0 else pltpu.roll(c, dd, 0)
            e = jnp.exp(jnp.minimum(c - c_sh, 0.0))
            sd = jnp.sum(q * k_sh * e, axis=-1, keepdims=True)
            ok = (cmr == -dd) & ((row & (GLA_DIAG - 1)) >= dd)
            scores = scores + jnp.where(ok, sd, 0.0)

        o = o + jnp.dot(scores.astype(BF16), v_bf, preferred_element_type=F32)

        lh = last[:, ksl]
        k_dec = (k * jnp.exp(lh - c)).astype(BF16)
        st_ref[h] = st * jnp.exp(lh) + _dot_tn(v_bf, k_dec)

        ms = jnp.mean(o * o, axis=-1, keepdims=True)
        y = (o * lax.rsqrt(ms + eps)) * ng_ref[...]
        g_out = gg_ref[:, vsl]
        o_ref[:, vsl] = (y * (g_out * jax.nn.sigmoid(g_out))).astype(o_ref.dtype)


def gla(proj, glr, w2, b_gate, norm_g, bsz, t, *, tt=128, eps=1e-6):
    m = proj.shape[0]
    nt = t // tt
    hk = GLA_HEADS * GLA_DK
    row = lambda b, i: b * nt + i
    vmem = (2 * (2 * tt * hk * 4 + 2 * tt * GROUP_WIDTH * 4 + tt * LANES * 4 + tt * GROUP_WIDTH * 2)
            + GLA_HEADS * GLA_DV * GLA_DK * 4 + 24 * tt * hk * 4 + VMEM_SLACK)
    return pl.pallas_call(
        functools.partial(_gla_kernel, tt=tt, eps=eps),
        grid=(bsz, nt),
        in_specs=[pl.BlockSpec((tt, hk), lambda b, i: (row(b, i), OFF_GQ // hk)),
                  pl.BlockSpec((tt, hk), lambda b, i: (row(b, i), OFF_GK // hk)),
                  pl.BlockSpec((tt, GROUP_WIDTH), lambda b, i: (row(b, i), OFF_GV // GROUP_WIDTH)),
                  pl.BlockSpec((tt, GROUP_WIDTH), lambda b, i: (row(b, i), OFF_GG // GROUP_WIDTH)),
                  pl.BlockSpec((tt, LANES), lambda b, i: (row(b, i), 0)),
                  pl.BlockSpec((LANES, hk), lambda b, i: (0, 0)),
                  pl.BlockSpec((1, hk), lambda b, i: (0, 0)),
                  pl.BlockSpec((1, GLA_DV), lambda b, i: (0, 0))],
        out_specs=pl.BlockSpec((tt, GROUP_WIDTH), lambda b, i: (row(b, i), 0)),
        out_shape=jax.ShapeDtypeStruct((m, GROUP_WIDTH), BF16),
        scratch_shapes=[pltpu.VMEM((GLA_HEADS, GLA_DV, GLA_DK), F32)],
        compiler_params=_params(("parallel", "arbitrary"), vmem),
        name="gla",
    )(proj, proj, proj, proj, glr, w2, b_gate.reshape(1, hk), norm_g.reshape(1, GLA_DV))


def _lru_kernel(lx_ref, lg_ref, cw_ref, cb_ref, wax_ref, ba_ref, bx_ref, lam_ref, o_ref,
                ext_ref, h_ref, *, tt):
    pad = SUBLANES

    @pl.when(pl.program_id(1) == 0)
    def _():
        ext_ref[0:pad, :] = jnp.zeros((pad, ext_ref.shape[1]), F32)
        h_ref[...] = jnp.zeros_like(h_ref)

    ext_ref[pad:pad + tt, :] = lx_ref[...]
    xc = cb_ref[...] + cw_ref[0:1, :] * ext_ref[pl.ds(pad - LRU_CONV + 1, tt), :]
    for kk in range(1, LRU_CONV):
        xc = xc + cw_ref[kk:kk + 1, :] * ext_ref[pl.ds(pad - LRU_CONV + 1 + kk, tt), :]
    ext_ref[0:pad, :] = ext_ref[tt:tt + pad, :]

    bd = LRU_BLOCK_DIM
    r_parts, i_parts = [], []
    for n in range(LRU_BLOCKS):
        ri = jnp.dot(xc[:, n * bd:(n + 1) * bd].astype(BF16), wax_ref[n], preferred_element_type=F32)
        r_parts.append(ri[:, :bd])
        i_parts.append(ri[:, bd:])
    r = jax.nn.sigmoid(jnp.concatenate(r_parts, axis=1) + ba_ref[...])
    ig = jax.nn.sigmoid(jnp.concatenate(i_parts, axis=1) + bx_ref[...])

    lam = lam_ref[...]
    softplus_neg_lam = jnp.maximum(-lam, 0.0) + jnp.log1p(jnp.exp(-jnp.abs(lam)))
    log_a = (-LRU_C) * r * softplus_neg_lam
    a = jnp.exp(log_a)
    u = jnp.sqrt(-jnp.tanh(log_a) * (a * a + 1.0)) * (ig * xc)

    rowi = _iota2(a.shape, 0)
    sft = 1
    while sft < tt:
        keep = rowi >= sft
        a_prev = jnp.where(keep, pltpu.roll(a, sft, 0), 1.0)
        u_prev = jnp.where(keep, pltpu.roll(u, sft, 0), 0.0)
        u = a * u_prev + u
        a = a * a_prev
        sft *= 2
    h = u + a * h_ref[0:1, :]
    h_ref[0:1, :] = h[tt - 1:tt, :]

    lg = lg_ref[...]
    gelu = 0.5 * lg * (1.0 + jnp.tanh(0.7978845608028654 * (lg + 0.044715 * (lg * lg * lg))))
    o_ref[...] = (h * gelu).astype(o_ref.dtype)


def rglru(proj, conv_w, conv_b, wax, b_a, b_x, lam, bsz, t, *, tt=256):
    m = proj.shape[0]
    nt = t // tt
    w = GROUP_WIDTH
    row = lambda b, i: b * nt + i
    vec = lambda: pl.BlockSpec((1, w), lambda b, i: (0, 0))
    vmem = 2 * (2 * tt * w * 4 + tt * w * 2) + 24 * tt * w * 4 + VMEM_SLACK
    return pl.pallas_call(
        functools.partial(_lru_kernel, tt=tt),
        grid=(bsz, nt),
        in_specs=[pl.BlockSpec((tt, w), lambda b, i: (row(b, i), OFF_LX // w)),
                  pl.BlockSpec((tt, w), lambda b, i: (row(b, i), OFF_LG // w)),
                  pl.BlockSpec((LRU_CONV, w), lambda b, i: (0, 0)),
                  vec(),
                  pl.BlockSpec((LRU_BLOCKS, LRU_BLOCK_DIM, 2 * LRU_BLOCK_DIM), lambda b, i: (0, 0, 0)),
                  vec(), vec(), vec()],
        out_specs=pl.BlockSpec((tt, w), lambda b, i: (row(b, i), 0)),
        out_shape=jax.ShapeDtypeStruct((m, w), BF16),
        scratch_shapes=[pltpu.VMEM((tt + SUBLANES, w), F32), pltpu.VMEM((SUBLANES, w), F32)],
        compiler_params=_params(("parallel", "arbitrary"), vmem),
        name="rglru",
    )(proj, proj, conv_w, conv_b.reshape(1, w), wax, b_a.reshape(1, w), b_x.reshape(1, w),
      lam.reshape(1, w))


CONF_PAD = 32


def _conformer_kernel(val_ref, gte_ref, cw_ref, cb_ref, g_ref, b_ref, o_ref, ext_ref, sh_ref,
                      *, tt, eps):
    pad = CONF_PAD
    rows = tt + pad

    @pl.when(pl.program_id(1) == 0)
    def _():
        ext_ref[0:pad, :] = jnp.zeros((pad, ext_ref.shape[1]), F32)

    ext_ref[pad:pad + tt, :] = val_ref[...] * jax.nn.sigmoid(gte_ref[...])
    ext = ext_ref[...]
    for b in range(1, SUBLANES):
        sh_ref[b - 1] = pltpu.roll(ext, rows - b, 0)
    base = pad - CONV_KERNEL + 1
    y = cb_ref[...]
    for kk in range(CONV_KERNEL):
        a, b = divmod(base + kk, SUBLANES)
        src = ext_ref if b == 0 else sh_ref.at[b - 1]
        y = y + cw_ref[kk:kk + 1, :] * src[pl.ds(a * SUBLANES, tt), :]
    ext_ref[0:pad, :] = ext_ref[tt:tt + pad, :]

    mu = jnp.mean(y, axis=-1, keepdims=True)
    yc = y - mu
    var = jnp.mean(yc * yc, axis=-1, keepdims=True)
    z = (yc * lax.rsqrt(var + eps)) * g_ref[...] + b_ref[...]
    o_ref[...] = (z * jax.nn.sigmoid(z)).astype(o_ref.dtype)


def conformer(proj, conv_w, conv_b, ln_g, ln_b, bsz, t, *, tt=256, eps=1e-5):
    m = proj.shape[0]
    nt = t // tt
    w = GROUP_WIDTH
    row = lambda b, i: b * nt + i
    vec = lambda: pl.BlockSpec((1, w), lambda b, i: (0, 0))
    vmem = (2 * (2 * tt * w * 4 + tt * w * 2) + (8 * tt + SUBLANES * (tt + CONF_PAD)) * w * 4
            + VMEM_SLACK)
    return pl.pallas_call(
        functools.partial(_conformer_kernel, tt=tt, eps=eps),
        grid=(bsz, nt),
        in_specs=[pl.BlockSpec((tt, w), lambda b, i: (row(b, i), OFF_CU // w)),
                  pl.BlockSpec((tt, w), lambda b, i: (row(b, i), OFF_CU // w + 1)),
                  pl.BlockSpec((CONV_KERNEL, w), lambda b, i: (0, 0)),
                  vec(), vec(), vec()],
        out_specs=pl.BlockSpec((tt, w), lambda b, i: (row(b, i), 0)),
        out_shape=jax.ShapeDtypeStruct((m, w), BF16),
        scratch_shapes=[pltpu.VMEM((tt + CONF_PAD, w), F32),
                        pltpu.VMEM((SUBLANES - 1, tt + CONF_PAD, w), F32)],
        compiler_params=_params(("parallel", "arbitrary"), vmem),
        name="conformer",
    )(proj, proj, conv_w, conv_b.reshape(1, w), ln_g.reshape(1, w), ln_b.reshape(1, w))


def kernel(x, norm_mix_g, w_in, gla_w_gate2, gla_b_gate, gla_norm_g, lru_conv_w, lru_conv_b, lru_w_a, lru_b_a, lru_w_x, lru_b_x, lru_lambda, conf_conv_w, conf_conv_b, conf_ln_g, conf_ln_b, w_out, norm_ffn_g, ffn_w_gate, ffn_w_up, ffn_w_down, final_norm_g):
    bsz, t, d = x.shape
    depth = w_in.shape[0]
    xf = x.reshape(bsz * t, d)
    w_in_t = jnp.swapaxes(w_in, 1, 2)
    h = rmsnorm(xf, norm_mix_g[0], BF16)
    ssq = None
    for l in range(depth):
        rank0 = N_PROJ_GLA
        w2 = jnp.pad(gla_w_gate2[l], ((0, LANES - GLA_GATE_RANK), (0, 0))).astype(BF16)
        wax = jnp.concatenate([lru_w_a[l], lru_w_x[l]], axis=-1).astype(BF16)

        proj_gla = ws_matmul([h], [w_in_t], l, out_dtype=F32, tm=512, tn=1024, ncols=N_PROJ_GLA,
                             transposed=True, row_ssq=ssq, name="in_proj_gla")
        proj = ws_matmul([h], [w_in_t], l, out_dtype=F32, tm=512, tn=1024, col0=rank0,
                         ncols=N_PROJ_REST, shift=GLA_GATE_RANK, transposed=True, row_ssq=ssq,
                         name="in_proj_rest")
        glr = ws_matmul([h], [w_in_t], l, out_dtype=F32, tm=1024, tn=LANES, col0=rank0, ncols=LANES,
                        transposed=True, row_ssq=ssq, name="in_proj_rank")
        o_a = gla(proj_gla, glr, w2, gla_b_gate[l], gla_norm_g[l], bsz, t)
        o_b = stick_breaking(proj, bsz, t)
        o_c = rglru(proj, lru_conv_w[l], lru_conv_b[l], wax, lru_b_a[l], lru_b_x[l], lru_lambda[l], bsz, t)
        o_d = conformer(proj, conf_conv_w[l], conf_conv_b[l], conf_ln_g[l], conf_ln_b[l], bsz, t)
        xf, h, ssq = ws_matmul([o_a, o_b, o_c, o_d], [w_out], l, xf, out_dtype=F32, tm=512, tn=1024,
                               next_gain=norm_ffn_g[l], name="out_proj")

        act, w_down = ws_matmul([h], [ffn_w_gate, ffn_w_up], l, out_dtype=BF16, tm=1024, tn=512,
                                swiglu=True, side=ffn_w_down, row_ssq=ssq, name="ffn_gate_up")
        if l + 1 < depth:
            xf, h, ssq = res_matmul(act, w_down, xf, tm=512, tn=512, next_gain=norm_mix_g[l + 1])
        else:
            xf = res_matmul(act, w_down, xf, tm=512, tn=512)
    return rmsnorm(xf, final_norm_g, F32).reshape(bsz, t, d)
```

```python
import functools

import jax
import jax.numpy as jnp
from jax import lax
from jax.experimental import pallas as pl
from jax.experimental.pallas import tpu as pltpu

F32 = jnp.float32
BF16 = jnp.bfloat16

GROUP_WIDTH = 1024
GLA_HEADS = 4
GLA_DK = 128
GLA_DV = 256
GLA_GATE_RANK = 16
GLA_GATE_TAU = 16.0
GLA_DIAG = 8
SB_HEADS = 8
SB_HEAD_DIM = 128
LRU_BLOCKS = 8
LRU_BLOCK_DIM = 128
LRU_CONV = 4
LRU_C = 8.0
CONV_KERNEL = 31
LANES = 128
SUBLANES = 8

OFF_GQ, OFF_GK, OFF_GV, OFF_GG = 0, 512, 1024, 2048
N_PROJ_GLA = 3072
OFF_SQ, OFF_SK, OFF_SV = 0, 1024, 2048
OFF_LX, OFF_LG, OFF_CU = 3072, 4096, 5120
N_PROJ_REST = 7168
SB_DEAD_LOG = -105.0

VMEM_SLACK = 6 << 20


def _params(sem, vmem_bytes):
    return pltpu.CompilerParams(dimension_semantics=sem, vmem_limit_bytes=int(vmem_bytes))


def _rmsnorm_kernel(x_ref, g_ref, o_ref, *, eps):
    x = x_ref[...]
    ms = jnp.mean(x * x, axis=-1, keepdims=True)
    o_ref[...] = ((x * lax.rsqrt(ms + eps)) * g_ref[...]).astype(o_ref.dtype)


def rmsnorm(x, g, out_dtype, tm=256, eps=1e-6):
    m, d = x.shape
    return pl.pallas_call(
        functools.partial(_rmsnorm_kernel, eps=eps),
        grid=(m // tm,),
        in_specs=[pl.BlockSpec((tm, d), lambda i: (i, 0)),
                  pl.BlockSpec((1, d), lambda i: (0, 0))],
        out_specs=pl.BlockSpec((tm, d), lambda i: (i, 0)),
        out_shape=jax.ShapeDtypeStruct((m, d), out_dtype),
        compiler_params=_params(("parallel",), 4 * tm * d * 4 + VMEM_SLACK),
        name="rmsnorm",
    )(x, g.reshape(1, d))


def _res_matmul_kernel(*refs, has_emit):
    a_ref, w_ref, res_ref = refs[:3]
    gain_ref = refs[3] if has_emit else None
    o_ref = refs[3 + has_emit]
    out = jnp.dot(a_ref[...], w_ref[...], preferred_element_type=F32) + res_ref[...]
    o_ref[...] = out
    if has_emit:
        _emit_norm_input(out, gain_ref, refs[5], refs[6])


def res_matmul(a, w, res, *, tm, tn, next_gain=None):
    m, k = a.shape
    n = w.shape[1]
    assert m % tm == 0 and n % tn == 0
    blk = lambda: pl.BlockSpec((tm, tn), lambda i, j: (i, j))
    in_specs = [pl.BlockSpec((tm, k), lambda i, j: (i, 0)),
                pl.BlockSpec((k, tn), lambda i, j: (0, j)), blk()]
    args = [a, w, res]
    out_specs = [blk()]
    out_shape = [jax.ShapeDtypeStruct((m, n), F32)]
    if next_gain is not None:
        in_specs.append(pl.BlockSpec((1, tn), lambda i, j: (0, j)))
        args.append(next_gain.reshape(1, n))
        out_specs += [blk(), pl.BlockSpec((tm, LANES), lambda i, j: (i, j))]
        out_shape += [jax.ShapeDtypeStruct((m, n), BF16),
                      jax.ShapeDtypeStruct((m, (n // tn) * LANES), F32)]
    vmem = 2 * (tm * k * 2 + k * tn * 2 + tm * tn * 10 + tm * LANES * 4) + 2 * tm * tn * 4 + VMEM_SLACK
    outs = pl.pallas_call(
        functools.partial(_res_matmul_kernel, has_emit=next_gain is not None),
        grid=(m // tm, n // tn),
        in_specs=in_specs,
        out_specs=out_specs,
        out_shape=out_shape,
        compiler_params=_params(("parallel", "arbitrary"), vmem),
        name="down_proj",
    )(*args)
    return outs[0] if len(outs) == 1 else tuple(outs)


def _row_scale(ssq_ref, inv_d, eps):
    parts = ssq_ref.shape[1] // LANES
    ssq = ssq_ref[:, 0:1]
    for p in range(1, parts):
        ssq = ssq + ssq_ref[:, p * LANES:p * LANES + 1]
    return lax.rsqrt(ssq * inv_d + eps)


def _emit_norm_input(out, gain_ref, xg_ref, ssq_ref):
    xg_ref[...] = (out * gain_ref[...]).astype(xg_ref.dtype)
    ssq_ref[...] = jnp.broadcast_to(jnp.sum(out * out, axis=-1, keepdims=True), ssq_ref.shape)


def _ws_matmul_kernel(*refs, n_a, n_w, ncb, kc, last_width, transposed, shift, has_res, has_side,
                      has_scale, has_emit, swiglu, inv_d, eps):
    a_refs = refs[:n_a]
    w_refs = refs[n_a:n_a + n_w]
    pos = n_a + n_w
    x_refs = refs[pos:pos + n_w] if shift else (None,) * n_w
    pos += n_w if shift else 0
    res_ref = refs[pos] if has_res else None
    pos += has_res
    side_ref = refs[pos] if has_side else None
    pos += has_side
    scale_ref = refs[pos] if has_scale else None
    pos += has_scale
    gain_ref = refs[pos] if has_emit else None
    pos += has_emit
    o_ref = refs[pos]
    pos += 1
    if has_side:
        refs[pos][...] = side_ref[...].astype(BF16)
        pos += 1
    xg_ref, ssq_ref = (refs[pos], refs[pos + 1]) if has_emit else (None, None)
    pos += 2 * has_emit
    wb_slots = (refs[pos:pos + n_w], refs[pos + n_w:pos + 2 * n_w])
    j = pl.program_id(0)
    i = pl.program_id(1)

    def convert_chunk(slot):
        rows = pl.ds(pl.multiple_of(i * kc, kc), kc)
        for w_ref, x_ref, wb_ref in zip(w_refs, x_refs, wb_slots[slot]):
            if not transposed:
                w = w_ref[...]
            elif shift:
                w = jnp.concatenate([w_ref[shift:, :], x_ref[...]], axis=0).T
            else:
                w = w_ref[...].T
            wb_ref[rows, :] = w.astype(BF16)

    def multiply(slot, width):
        ka = a_refs[0].shape[1]
        outs = []
        for wb_ref in wb_slots[slot]:
            acc = None
            for g, a_ref in enumerate(a_refs):
                d = jnp.dot(a_ref[...], wb_ref[g * ka:(g + 1) * ka, :width],
                            preferred_element_type=F32)
                acc = d if acc is None else acc + d
            outs.append(acc)
        if has_scale:
            r = _row_scale(scale_ref, inv_d, eps)
            outs = [acc * r for acc in outs]
        if swiglu:
            gate, up = outs
            out = (gate * jax.nn.sigmoid(gate)) * up
        else:
            out = outs[0]
        if has_res:
            out = out + res_ref[:, :width]
        o_ref[:, :width] = out.astype(o_ref.dtype)
        if has_emit:
            _emit_norm_input(out, gain_ref, xg_ref, ssq_ref)

    @pl.when(j == 0)
    def _():
        convert_chunk(0)

    tn = o_ref.shape[1]
    for parity in (0, 1):
        @pl.when((j > 0) & (j < ncb) & (j % 2 == parity))
        def _():
            convert_chunk(parity)
            multiply(1 - parity, tn)

    @pl.when(j == ncb)
    def _():
        multiply((ncb - 1) % 2, last_width)


def ws_matmul(a_list, w_list, layer, res=None, *, out_dtype, tm, tn, col0=0, ncols=None, shift=0,
              transposed=False, swiglu=False, side=None, row_ssq=None, next_gain=None, eps=1e-6,
              name="ws_matmul"):
    n_a, n_w = len(a_list), len(w_list)
    m, ka = a_list[0].shape
    if transposed:
        _, n_total, k_total = w_list[0].shape
    else:
        _, k_total, n_total = w_list[0].shape
    ncols = n_total if ncols is None else ncols
    nm = m // tm
    kc = k_total // nm
    assert ka * n_a == k_total and m % tm == 0 and col0 % tn == 0 and n_w == (2 if swiglu else 1)
    assert kc * nm == k_total and kc % LANES == 0
    assert shift % SUBLANES == 0 and tn % shift == 0 if shift else True
    assert transposed or not shift
    assert ncols % tn == 0 or (col0 + ncols == n_total and not shift)
    assert ncols % LANES == 0
    ncb = pl.cdiv(ncols, tn)
    cb = col0 // tn

    row_blk = lambda j, i: jnp.where(j > 0, i, 0)
    out_col = lambda j: jnp.maximum(j - 1, 0)
    w_col = lambda j: cb + jnp.minimum(j, ncb - 1)
    w_chunk = lambda j, i: jnp.where(j < ncb, i, nm - 1)

    in_specs = [pl.BlockSpec((tm, ka), lambda j, i: (row_blk(j, i), 0)) for _ in a_list]
    if transposed:
        in_specs += [pl.BlockSpec((None, tn, kc), lambda j, i: (layer, w_col(j), w_chunk(j, i)))
                     for _ in w_list]
    else:
        in_specs += [pl.BlockSpec((None, kc, tn), lambda j, i: (layer, w_chunk(j, i), w_col(j)))
                     for _ in w_list]
    args = list(a_list) + list(w_list)
    if shift:
        per = tn // shift
        in_specs += [pl.BlockSpec((None, shift, kc),
                                  lambda j, i: (layer, (w_col(j) + 1) * per, w_chunk(j, i)))
                     for _ in w_list]
        args += list(w_list)
    if res is not None:
        in_specs.append(pl.BlockSpec((tm, tn), lambda j, i: (row_blk(j, i), out_col(j))))
        args.append(res)
    out_specs = [pl.BlockSpec((tm, tn), lambda j, i: (row_blk(j, i), out_col(j)))]
    out_shape = [jax.ShapeDtypeStruct((m, ncols), out_dtype)]
    side_bytes = 0
    if side is not None:
        _, srows, scols = side.shape
        bf16_rows = 2 * SUBLANES
        slab = next(s for s in range(bf16_rows, srows + 1, bf16_rows)
                    if srows % s == 0 and srows // s <= (ncb + 1) * nm)
        last = srows // slab - 1
        in_specs.append(pl.BlockSpec((None, slab, scols),
                                     lambda j, i: (layer, jnp.minimum(j * nm + i, last), 0)))
        args.append(side)
        out_specs.append(pl.BlockSpec((slab, scols), lambda j, i: (jnp.minimum(j * nm + i, last), 0)))
        out_shape.append(jax.ShapeDtypeStruct((srows, scols), BF16))
        side_bytes = slab * scols * 6
    extra_bytes = 0
    if row_ssq is not None:
        in_specs.append(pl.BlockSpec((tm, row_ssq.shape[1]), lambda j, i: (row_blk(j, i), 0)))
        args.append(row_ssq)
        extra_bytes += tm * row_ssq.shape[1] * 4
    if next_gain is not None:
        assert ncols % tn == 0
        in_specs.append(pl.BlockSpec((1, tn), lambda j, i: (0, out_col(j))))
        args.append(next_gain.reshape(1, ncols))
        out_specs.append(pl.BlockSpec((tm, tn), lambda j, i: (row_blk(j, i), out_col(j))))
        out_shape.append(jax.ShapeDtypeStruct((m, ncols), BF16))
        out_specs.append(pl.BlockSpec((tm, LANES), lambda j, i: (row_blk(j, i), out_col(j))))
        out_shape.append(jax.ShapeDtypeStruct((m, ncb * LANES), F32))
        extra_bytes += tm * tn * 2 + tm * LANES * 4
    osz = jnp.dtype(out_dtype).itemsize
    vmem = (2 * (n_a * tm * ka * 2 + n_w * kc * (tn + shift) * 4 + tm * tn * osz
                 + (tm * tn * 4 if res is not None else 0) + side_bytes + extra_bytes)
            + 2 * n_w * k_total * tn * 2 + (n_w + 1) * tm * tn * 4 + VMEM_SLACK)
    outs = pl.pallas_call(
        functools.partial(_ws_matmul_kernel, n_a=n_a, n_w=n_w, ncb=ncb, kc=kc,
                          last_width=ncols - (ncb - 1) * tn, transposed=transposed,
                          shift=shift, has_res=res is not None, has_side=side is not None,
                          has_scale=row_ssq is not None, has_emit=next_gain is not None,
                          swiglu=swiglu, inv_d=1.0 / k_total, eps=eps),
        grid=(ncb + 1, nm),
        in_specs=in_specs,
        out_specs=out_specs,
        out_shape=out_shape,
        scratch_shapes=[pltpu.VMEM((k_total, tn), BF16) for _ in range(2 * n_w)],
        compiler_params=_params(("arbitrary", "arbitrary"), vmem),
        name=name,
    )(*args)
    return outs[0] if len(outs) == 1 else tuple(outs)


def _softplus_parts(z):
    l = jnp.log(1.0 + jnp.exp(-jnp.abs(z)))
    return jnp.minimum(z, 0.0) - l, -jnp.maximum(z, 0.0) - l


def _split_bf16(x):
    hi = x.astype(BF16)
    lo = (x - hi.astype(F32)).astype(BF16)
    return hi, lo


def _dot_nt(a, b):
    return lax.dot_general(a, b, (((1,), (1,)), ((), ())), preferred_element_type=F32)


def _dot_tn(a, b):
    return lax.dot_general(a, b, (((0,), (0,)), ((), ())), preferred_element_type=F32)


def _iota2(shape, dim):
    return lax.broadcasted_iota(jnp.int32, shape, dim)


def _sb_kernel(q_ref, k_ref, v_ref, o_ref, *, tb, scale):
    i = pl.program_id(2)
    hd = SB_HEAD_DIM
    nh = q_ref.shape[1] // hd
    row = _iota2((tb, tb), 0)
    col = _iota2((tb, tb), 1)
    tri = jnp.where(row > col, 1.0, 0.0).astype(BF16)
    before = col < row
    qs = [q_ref[:, h * hd:(h + 1) * hd].astype(BF16) for h in range(nh)]

    def blocks(ks, carries, accs, diagonal):
        log_betas, log_rests, pieces = [], [], []
        for h in range(nh):
            kb = k_ref[pl.ds(ks, tb), h * hd:(h + 1) * hd].astype(BF16)
            log_beta, log_rest = _softplus_parts(_dot_nt(qs[h], kb) * scale)
            if diagonal:
                log_rest = jnp.where(before, log_rest, 0.0)
            log_betas.append(log_beta)
            log_rests.append(log_rest)
            pieces += list(_split_bf16(log_rest))
        tails = jnp.dot(jnp.concatenate(pieces, axis=0), tri, preferred_element_type=F32)
        new_carries, new_accs = [], []
        for h in range(nh):
            tail = tails[2 * h * tb:(2 * h + 1) * tb] + tails[(2 * h + 1) * tb:(2 * h + 2) * tb]
            w = jnp.exp(log_betas[h] + tail + carries[h])
            if diagonal:
                w = jnp.where(before, w, 0.0)
            vb = v_ref[pl.ds(ks, tb), h * hd:(h + 1) * hd].astype(BF16)
            new_accs.append(accs[h] + jnp.dot(w.astype(BF16), vb, preferred_element_type=F32))
            new_carries.append(carries[h] + jnp.sum(log_rests[h], axis=-1, keepdims=True))
        return tuple(new_carries), tuple(new_accs)

    def alive(carries):
        top = carries[0]
        for c in carries[1:]:
            top = jnp.maximum(top, c)
        return (jnp.max(top) >= SB_DEAD_LOG).astype(jnp.int32)

    k0 = pl.multiple_of(i * tb, tb)
    carries, accs = blocks(k0, (jnp.zeros((tb, 1), F32),) * nh, (jnp.zeros((tb, hd), F32),) * nh,
                           True)

    def cond(st):
        return (st[0] < i) & (st[1] > 0)

    def body(st):
        jb, _, carries, accs = st
        ks = pl.multiple_of((i - 1 - jb) * tb, tb)
        carries, accs = blocks(ks, carries, accs, False)
        return jb + 1, alive(carries), carries, accs

    _, _, _, accs = lax.while_loop(cond, body, (jnp.int32(0), alive(carries), carries, accs))
    for h in range(nh):
        o_ref[:, h * hd:(h + 1) * hd] = accs[h].astype(o_ref.dtype)


def stick_breaking(proj, bsz, t, *, tb=256, heads_per_step=4):
    m = proj.shape[0]
    nq = t // tb
    wd = heads_per_step * SB_HEAD_DIM
    cq, ck, cv = OFF_SQ // wd, OFF_SK // wd, OFF_SV // wd
    vmem = (2 * (tb * wd * 4 + 2 * t * wd * 4 + tb * wd * 2)
            + heads_per_step * 12 * tb * tb * 4 + VMEM_SLACK)
    return pl.pallas_call(
        functools.partial(_sb_kernel, tb=tb, scale=SB_HEAD_DIM ** -0.5),
        grid=(bsz, SB_HEADS // heads_per_step, nq),
        in_specs=[pl.BlockSpec((tb, wd), lambda b, h, i: (b * nq + i, cq + h)),
                  pl.BlockSpec((t, wd), lambda b, h, i: (b, ck + h)),
                  pl.BlockSpec((t, wd), lambda b, h, i: (b, cv + h))],
        out_specs=pl.BlockSpec((tb, wd), lambda b, h, i: (b * nq + i, h)),
        out_shape=jax.ShapeDtypeStruct((m, GROUP_WIDTH), BF16),
        compiler_params=_params(("parallel", "parallel", "arbitrary"), vmem),
        name="stick_breaking",
    )(proj, proj, proj)


def _gla_kernel(gq_ref, gk_ref, gv_ref, gg_ref, glr_ref, w2_ref, bg_ref, ng_ref, o_ref, st_ref,
                *, tt, eps):
    @pl.when(pl.program_id(1) == 0)
    def _():
        st_ref[...] = jnp.zeros_like(st_ref)

    row = _iota2((tt, tt), 0)
    col = _iota2((tt, tt), 1)
    cmr = col - row

    gate = jnp.dot(glr_ref[...].astype(BF16), w2_ref[...], preferred_element_type=F32) + bg_ref[...]
    log_alpha = _softplus_parts(gate)[0] * (1.0 / GLA_GATE_TAU)
    la_hi, la_lo = _split_bf16(log_alpha)

    def rowsum(sel):
        s = sel.astype(BF16)
        return (jnp.dot(s, la_hi, preferred_element_type=F32)
                + jnp.dot(s, la_lo, preferred_element_type=F32))

    cum = rowsum(jnp.where(col <= row, 1.0, 0.0))
    last = cum[tt - 1:tt, :]

    levels = []
    s = tt // 2
    while s >= GLA_DIAG:
        start_r = (row & ~(2 * s - 1)) + s
        is_r = (row & s) != 0
        sel = jnp.where(is_r,
                        jnp.where((col >= start_r) & (col <= row), 1.0, 0.0),
                        jnp.where((col > row) & (col < start_r), 1.0, 0.0))
        same_pair = (row & ~(2 * s - 1)) == (col & ~(2 * s - 1))
        levels.append((s, jnp.exp(rowsum(sel)), same_pair))
        s //= 2

    q_all = gq_ref[...] * (GLA_DK ** -0.5)
    k_all = gk_ref[...]
    v_all = gv_ref[...]
    rowl = _iota2((tt, GLA_DK), 0)
    dv = GLA_DV
    for h in range(GLA_HEADS):
        ksl = slice(h * GLA_DK, (h + 1) * GLA_DK)
        vsl = slice(h * dv, (h + 1) * dv)
        q, k, c = q_all[:, ksl], k_all[:, ksl], cum[:, ksl]
        v_bf = v_all[:, vsl].astype(BF16)
        st = st_ref[h]

        o = _dot_nt((q * jnp.exp(c)).astype(BF16), st.astype(BF16))

        scores = jnp.zeros((tt, tt), F32)
        for s, decay, same_pair in levels:
            d = decay[:, ksl]
            is_r = (rowl & s) != 0
            qh = jnp.where(is_r, q * d, 0.0).astype(BF16)
            kh = jnp.where(is_r, 0.0, k * d).astype(BF16)
            sc = _dot_nt(qh, kh)
            scores = scores + (sc if 2 * s == tt else jnp.where(same_pair, sc, 0.0))

        for dd in range(GLA_DIAG):
            k_sh = k if dd == 0 else pltpu.roll(k, dd, 0)
            c_sh = c if dd == 0 else pltpu.roll(c, dd, 0)
            e = jnp.exp(jnp.minimum(c - c_sh, 0.0))
            sd = jnp.sum(q * k_sh * e, axis=-1, keepdims=True)
            ok = (cmr == -dd) & ((row & (GLA_DIAG - 1)) >= dd)
            scores = scores + jnp.where(ok, sd, 0.0)

        o = o + jnp.dot(scores.astype(BF16), v_bf, preferred_element_type=F32)

        lh = last[:, ksl]
        k_dec = (k * jnp.exp(lh - c)).astype(BF16)
        st_ref[h] = st * jnp.exp(lh) + _dot_tn(v_bf, k_dec)

        ms = jnp.mean(o * o, axis=-1, keepdims=True)
        y = (o * lax.rsqrt(ms + eps)) * ng_ref[...]
        g_out = gg_ref[:, vsl]
        o_ref[:, vsl] = (y * (g_out * jax.nn.sigmoid(g_out))).astype(o_ref.dtype)


def gla(proj, glr, w2, b_gate, norm_g, bsz, t, *, tt=128, eps=1e-6):
    m = proj.shape[0]
    nt = t // tt
    hk = GLA_HEADS * GLA_DK
    row = lambda b, i: b * nt + i
    vmem = (2 * (2 * tt * hk * 4 + 2 * tt * GROUP_WIDTH * 4 + tt * LANES * 4 + tt * GROUP_WIDTH * 2)
            + GLA_HEADS * GLA_DV * GLA_DK * 4 + 24 * tt * hk * 4 + VMEM_SLACK)
    return pl.pallas_call(
        functools.partial(_gla_kernel, tt=tt, eps=eps),
        grid=(bsz, nt),
        in_specs=[pl.BlockSpec((tt, hk), lambda b, i: (row(b, i), OFF_GQ // hk)),
                  pl.BlockSpec((tt, hk), lambda b, i: (row(b, i), OFF_GK // hk)),
                  pl.BlockSpec((tt, GROUP_WIDTH), lambda b, i: (row(b, i), OFF_GV // GROUP_WIDTH)),
                  pl.BlockSpec((tt, GROUP_WIDTH), lambda b, i: (row(b, i), OFF_GG // GROUP_WIDTH)),
                  pl.BlockSpec((tt, LANES), lambda b, i: (row(b, i), 0)),
                  pl.BlockSpec((LANES, hk), lambda b, i: (0, 0)),
                  pl.BlockSpec((1, hk), lambda b, i: (0, 0)),
                  pl.BlockSpec((1, GLA_DV), lambda b, i: (0, 0))],
        out_specs=pl.BlockSpec((tt, GROUP_WIDTH), lambda b, i: (row(b, i), 0)),
        out_shape=jax.ShapeDtypeStruct((m, GROUP_WIDTH), BF16),
        scratch_shapes=[pltpu.VMEM((GLA_HEADS, GLA_DV, GLA_DK), F32)],
        compiler_params=_params(("parallel", "arbitrary"), vmem),
        name="gla",
    )(proj, proj, proj, proj, glr, w2, b_gate.reshape(1, hk), norm_g.reshape(1, GLA_DV))


def _lru_kernel(lx_ref, lg_ref, cw_ref, cb_ref, wax_ref, ba_ref, bx_ref, lam_ref, o_ref,
                ext_ref, h_ref, *, tt):
    pad = SUBLANES

    @pl.when(pl.program_id(1) == 0)
    def _():
        ext_ref[0:pad, :] = jnp.zeros((pad, ext_ref.shape[1]), F32)
        h_ref[...] = jnp.zeros_like(h_ref)

    ext_ref[pad:pad + tt, :] = lx_ref[...]
    xc = cb_ref[...] + cw_ref[0:1, :] * ext_ref[pl.ds(pad - LRU_CONV + 1, tt), :]
    for kk in range(1, LRU_CONV):
        xc = xc + cw_ref[kk:kk + 1, :] * ext_ref[pl.ds(pad - LRU_CONV + 1 + kk, tt), :]
    ext_ref[0:pad, :] = ext_ref[tt:tt + pad, :]

    bd = LRU_BLOCK_DIM
    r_parts, i_parts = [], []
    for n in range(LRU_BLOCKS):
        ri = jnp.dot(xc[:, n * bd:(n + 1) * bd].astype(BF16), wax_ref[n], preferred_element_type=F32)
        r_parts.append(ri[:, :bd])
        i_parts.append(ri[:, bd:])
    r = jax.nn.sigmoid(jnp.concatenate(r_parts, axis=1) + ba_ref[...])
    ig = jax.nn.sigmoid(jnp.concatenate(i_parts, axis=1) + bx_ref[...])

    lam = lam_ref[...]
    softplus_neg_lam = jnp.maximum(-lam, 0.0) + jnp.log1p(jnp.exp(-jnp.abs(lam)))
    log_a = (-LRU_C) * r * softplus_neg_lam
    a = jnp.exp(log_a)
    u = jnp.exp(0.5 * jnp.log(-jnp.tanh(log_a) * (a * a + 1.0))) * (ig * xc)

    rowi = _iota2(a.shape, 0)
    sft = 1
    while sft < tt:
        keep = rowi >= sft
        a_prev = jnp.where(keep, pltpu.roll(a, sft, 0), 1.0)
        u_prev = jnp.where(keep, pltpu.roll(u, sft, 0), 0.0)
        u = a * u_prev + u
        a = a * a_prev
        sft *= 2
    h = u + a * h_ref[0:1, :]
    h_ref[0:1, :] = h[tt - 1:tt, :]

    lg = lg_ref[...]
    gelu = 0.5 * lg * (1.0 + jnp.tanh(0.7978845608028654 * (lg + 0.044715 * (lg * lg * lg))))
    o_ref[...] = (h * gelu).astype(o_ref.dtype)


def rglru(proj, conv_w, conv_b, wax, b_a, b_x, lam, bsz, t, *, tt=256):
    m = proj.shape[0]
    nt = t // tt
    w = GROUP_WIDTH
    row = lambda b, i: b * nt + i
    vec = lambda: pl.BlockSpec((1, w), lambda b, i: (0, 0))
    vmem = 2 * (2 * tt * w * 4 + tt * w * 2) + 24 * tt * w * 4 + VMEM_SLACK
    return pl.pallas_call(
        functools.partial(_lru_kernel, tt=tt),
        grid=(bsz, nt),
        in_specs=[pl.BlockSpec((tt, w), lambda b, i: (row(b, i), OFF_LX // w)),
                  pl.BlockSpec((tt, w), lambda b, i: (row(b, i), OFF_LG // w)),
                  pl.BlockSpec((LRU_CONV, w), lambda b, i: (0, 0)),
                  vec(),
                  pl.BlockSpec((LRU_BLOCKS, LRU_BLOCK_DIM, 2 * LRU_BLOCK_DIM), lambda b, i: (0, 0, 0)),
                  vec(), vec(), vec()],
        out_specs=pl.BlockSpec((tt, w), lambda b, i: (row(b, i), 0)),
        out_shape=jax.ShapeDtypeStruct((m, w), BF16),
        scratch_shapes=[pltpu.VMEM((tt + SUBLANES, w), F32), pltpu.VMEM((SUBLANES, w), F32)],
        compiler_params=_params(("parallel", "arbitrary"), vmem),
        name="rglru",
    )(proj, proj, conv_w, conv_b.reshape(1, w), wax, b_a.reshape(1, w), b_x.reshape(1, w),
      lam.reshape(1, w))


CONF_PAD = 32


def _conformer_kernel(val_ref, gte_ref, cw_ref, cb_ref, g_ref, b_ref, o_ref, ext_ref, sh_ref,
                      *, tt, eps):
    pad = CONF_PAD
    rows = tt + pad

    @pl.when(pl.program_id(1) == 0)
    def _():
        ext_ref[0:pad, :] = jnp.zeros((pad, ext_ref.shape[1]), F32)

    ext_ref[pad:pad + tt, :] = val_ref[...] * jax.nn.sigmoid(gte_ref[...])
    ext = ext_ref[...]
    for b in range(1, SUBLANES):
        sh_ref[b - 1] = pltpu.roll(ext, rows - b, 0)
    base = pad - CONV_KERNEL + 1
    y = cb_ref[...]
    for kk in range(CONV_KERNEL):
        a, b = divmod(base + kk, SUBLANES)
        src = ext_ref if b == 0 else sh_ref.at[b - 1]
        y = y + cw_ref[kk:kk + 1, :] * src[pl.ds(a * SUBLANES, tt), :]
    ext_ref[0:pad, :] = ext_ref[tt:tt + pad, :]

    mu = jnp.mean(y, axis=-1, keepdims=True)
    yc = y - mu
    var = jnp.mean(yc * yc, axis=-1, keepdims=True)
    z = (yc * lax.rsqrt(var + eps)) * g_ref[...] + b_ref[...]
    o_ref[...] = (z * jax.nn.sigmoid(z)).astype(o_ref.dtype)


def conformer(proj, conv_w, conv_b, ln_g, ln_b, bsz, t, *, tt=256, eps=1e-5):
    m = proj.shape[0]
    nt = t // tt
    w = GROUP_WIDTH
    row = lambda b, i: b * nt + i
    vec = lambda: pl.BlockSpec((1, w), lambda b, i: (0, 0))
    vmem = (2 * (2 * tt * w * 4 + tt * w * 2) + (8 * tt + SUBLANES * (tt + CONF_PAD)) * w * 4
            + VMEM_SLACK)
    return pl.pallas_call(
        functools.partial(_conformer_kernel, tt=tt, eps=eps),
        grid=(bsz, nt),
        in_specs=[pl.BlockSpec((tt, w), lambda b, i: (row(b, i), OFF_CU // w)),
                  pl.BlockSpec((tt, w), lambda b, i: (row(b, i), OFF_CU // w + 1)),
                  pl.BlockSpec((CONV_KERNEL, w), lambda b, i: (0, 0)),
                  vec(), vec(), vec()],
        out_specs=pl.BlockSpec((tt, w), lambda b, i: (row(b, i), 0)),
        out_shape=jax.ShapeDtypeStruct((m, w), BF16),
        scratch_shapes=[pltpu.VMEM((tt + CONF_PAD, w), F32),
                        pltpu.VMEM((SUBLANES - 1, tt + CONF_PAD, w), F32)],
        compiler_params=_params(("parallel", "arbitrary"), vmem),
        name="conformer",
    )(proj, proj, conv_w, conv_b.reshape(1, w), ln_g.reshape(1, w), ln_b.reshape(1, w))


def kernel(x, norm_mix_g, w_in, gla_w_gate2, gla_b_gate, gla_norm_g, lru_conv_w, lru_conv_b, lru_w_a, lru_b_a, lru_w_x, lru_b_x, lru_lambda, conf_conv_w, conf_conv_b, conf_ln_g, conf_ln_b, w_out, norm_ffn_g, ffn_w_gate, ffn_w_up, ffn_w_down, final_norm_g):
    bsz, t, d = x.shape
    depth = w_in.shape[0]
    xf = x.reshape(bsz * t, d)
    w_in_t = jnp.swapaxes(w_in, 1, 2)
    h = rmsnorm(xf, norm_mix_g[0], BF16)
    ssq = None
    for l in range(depth):
        rank0 = N_PROJ_GLA
        w2 = jnp.pad(gla_w_gate2[l], ((0, LANES - GLA_GATE_RANK), (0, 0))).astype(BF16)
        wax = jnp.concatenate([lru_w_a[l], lru_w_x[l]], axis=-1).astype(BF16)

        proj_gla = ws_matmul([h], [w_in_t], l, out_dtype=F32, tm=512, tn=1024, ncols=N_PROJ_GLA,
                             transposed=True, row_ssq=ssq, name="in_proj_gla")
        proj = ws_matmul([h], [w_in_t], l, out_dtype=F32, tm=512, tn=1024, col0=rank0,
                         ncols=N_PROJ_REST, shift=GLA_GATE_RANK, transposed=True, row_ssq=ssq,
                         name="in_proj_rest")
        glr = ws_matmul([h], [w_in_t], l, out_dtype=F32, tm=1024, tn=LANES, col0=rank0, ncols=LANES,
                        transposed=True, row_ssq=ssq, name="in_proj_rank")
        o_a = gla(proj_gla, glr, w2, gla_b_gate[l], gla_norm_g[l], bsz, t)
        o_b = stick_breaking(proj, bsz, t)
        o_c = rglru(proj, lru_conv_w[l], lru_conv_b[l], wax, lru_b_a[l], lru_b_x[l], lru_lambda[l], bsz, t)
        o_d = conformer(proj, conf_conv_w[l], conf_conv_b[l], conf_ln_g[l], conf_ln_b[l], bsz, t)
        xf, h, ssq = ws_matmul([o_a, o_b, o_c, o_d], [w_out], l, xf, out_dtype=F32, tm=512, tn=1024,
                               next_gain=norm_ffn_g[l], name="out_proj")

        act, w_down = ws_matmul([h], [ffn_w_gate, ffn_w_up], l, out_dtype=BF16, tm=1024, tn=512,
                                swiglu=True, side=ffn_w_down, row_ssq=ssq, name="ffn_gate_up")
        if l + 1 < depth:
            xf, h, ssq = res_matmul(act, w_down, xf, tm=512, tn=512, next_gain=norm_mix_g[l + 1])
        else:
            xf = res_matmul(act, w_down, xf, tm=512, tn=512)
    return rmsnorm(xf, final_norm_g, F32).reshape(bsz, t, d)
```

```python
import functools

import jax
import jax.numpy as jnp
from jax import lax
from jax.experimental import pallas as pl
from jax.experimental.pallas import tpu as pltpu

F32 = jnp.float32
BF16 = jnp.bfloat16

GROUP_WIDTH = 1024
GLA_HEADS = 4
GLA_DK = 128
GLA_DV = 256
GLA_GATE_RANK = 16
GLA_GATE_TAU = 16.0
GLA_DIAG = 4
SB_HEADS = 8
SB_HEAD_DIM = 128
LRU_BLOCKS = 8
LRU_BLOCK_DIM = 128
LRU_CONV = 4
LRU_C = 8.0
CONV_KERNEL = 31
LANES = 128
SUBLANES = 8

OFF_GQ, OFF_GK, OFF_GV, OFF_GG = 0, 512, 1024, 2048
N_PROJ_GLA = 3072
OFF_SQ, OFF_SK, OFF_SV = 0, 1024, 2048
OFF_LX, OFF_LG, OFF_CU = 3072, 4096, 5120
N_PROJ_REST = 7168
SB_DEAD_LOG = -105.0

VMEM_SLACK = 6 << 20


def _params(sem, vmem_bytes):
    return pltpu.CompilerParams(dimension_semantics=sem, vmem_limit_bytes=int(vmem_bytes))


def _rmsnorm_kernel(x_ref, g_ref, o_ref, *, eps):
    x = x_ref[...]
    ms = jnp.mean(x * x, axis=-1, keepdims=True)
    o_ref[...] = ((x * lax.rsqrt(ms + eps)) * g_ref[...]).astype(o_ref.dtype)


def rmsnorm(x, g, out_dtype, tm=256, eps=1e-6):
    m, d = x.shape
    return pl.pallas_call(
        functools.partial(_rmsnorm_kernel, eps=eps),
        grid=(m // tm,),
        in_specs=[pl.BlockSpec((tm, d), lambda i: (i, 0)),
                  pl.BlockSpec((1, d), lambda i: (0, 0))],
        out_specs=pl.BlockSpec((tm, d), lambda i: (i, 0)),
        out_shape=jax.ShapeDtypeStruct((m, d), out_dtype),
        compiler_params=_params(("parallel",), 4 * tm * d * 4 + VMEM_SLACK),
        name="rmsnorm",
    )(x, g.reshape(1, d))


def _res_matmul_kernel(*refs, has_emit):
    a_ref, w_ref, res_ref = refs[:3]
    gain_ref = refs[3] if has_emit else None
    o_ref = refs[3 + has_emit]
    out = jnp.dot(a_ref[...], w_ref[...], preferred_element_type=F32) + res_ref[...]
    o_ref[...] = out
    if has_emit:
        _emit_norm_input(out, gain_ref, refs[5], refs[6])


def res_matmul(a, w, res, *, tm, tn, next_gain=None):
    m, k = a.shape
    n = w.shape[1]
    assert m % tm == 0 and n % tn == 0
    blk = lambda: pl.BlockSpec((tm, tn), lambda i, j: (i, j))
    in_specs = [pl.BlockSpec((tm, k), lambda i, j: (i, 0)),
                pl.BlockSpec((k, tn), lambda i, j: (0, j)), blk()]
    args = [a, w, res]
    out_specs = [blk()]
    out_shape = [jax.ShapeDtypeStruct((m, n), F32)]
    if next_gain is not None:
        in_specs.append(pl.BlockSpec((1, tn), lambda i, j: (0, j)))
        args.append(next_gain.reshape(1, n))
        out_specs += [blk(), pl.BlockSpec((tm, LANES), lambda i, j: (i, j))]
        out_shape += [jax.ShapeDtypeStruct((m, n), BF16),
                      jax.ShapeDtypeStruct((m, (n // tn) * LANES), F32)]
    vmem = 2 * (tm * k * 2 + k * tn * 2 + tm * tn * 10 + tm * LANES * 4) + 2 * tm * tn * 4 + VMEM_SLACK
    outs = pl.pallas_call(
        functools.partial(_res_matmul_kernel, has_emit=next_gain is not None),
        grid=(m // tm, n // tn),
        in_specs=in_specs,
        out_specs=out_specs,
        out_shape=out_shape,
        compiler_params=_params(("parallel", "arbitrary"), vmem),
        name="down_proj",
    )(*args)
    return outs[0] if len(outs) == 1 else tuple(outs)


def _row_scale(ssq_ref, inv_d, eps):
    parts = ssq_ref.shape[1] // LANES
    ssq = ssq_ref[:, 0:1]
    for p in range(1, parts):
        ssq = ssq + ssq_ref[:, p * LANES:p * LANES + 1]
    return lax.rsqrt(ssq * inv_d + eps)


def _emit_norm_input(out, gain_ref, xg_ref, ssq_ref):
    xg_ref[...] = (out * gain_ref[...]).astype(xg_ref.dtype)
    ssq_ref[...] = jnp.broadcast_to(jnp.sum(out * out, axis=-1, keepdims=True), ssq_ref.shape)


def _ws_matmul_kernel(*refs, n_a, n_w, ncb, kc, last_width, transposed, shift, has_res, has_side,
                      has_scale, has_emit, swiglu, inv_d, eps):
    a_refs = refs[:n_a]
    w_refs = refs[n_a:n_a + n_w]
    pos = n_a + n_w
    x_refs = refs[pos:pos + n_w] if shift else (None,) * n_w
    pos += n_w if shift else 0
    res_ref = refs[pos] if has_res else None
    pos += has_res
    side_ref = refs[pos] if has_side else None
    pos += has_side
    scale_ref = refs[pos] if has_scale else None
    pos += has_scale
    gain_ref = refs[pos] if has_emit else None
    pos += has_emit
    o_ref = refs[pos]
    pos += 1
    if has_side:
        refs[pos][...] = side_ref[...].astype(BF16)
        pos += 1
    xg_ref, ssq_ref = (refs[pos], refs[pos + 1]) if has_emit else (None, None)
    pos += 2 * has_emit
    wb_slots = (refs[pos:pos + n_w], refs[pos + n_w:pos + 2 * n_w])
    j = pl.program_id(0)
    i = pl.program_id(1)

    def convert_chunk(slot):
        rows = pl.ds(pl.multiple_of(i * kc, kc), kc)
        for w_ref, x_ref, wb_ref in zip(w_refs, x_refs, wb_slots[slot]):
            if not transposed:
                w = w_ref[...]
            elif shift:
                w = jnp.concatenate([w_ref[shift:, :], x_ref[...]], axis=0).T
            else:
                w = w_ref[...].T
            wb_ref[rows, :] = w.astype(BF16)

    def multiply(slot, width):
        ka = a_refs[0].shape[1]
        outs = []
        for wb_ref in wb_slots[slot]:
            acc = None
            for g, a_ref in enumerate(a_refs):
                d = jnp.dot(a_ref[...], wb_ref[g * ka:(g + 1) * ka, :width],
                            preferred_element_type=F32)
                acc = d if acc is None else acc + d
            outs.append(acc)
        if has_scale:
            r = _row_scale(scale_ref, inv_d, eps)
            outs = [acc * r for acc in outs]
        if swiglu:
            gate, up = outs
            out = (gate * jax.nn.sigmoid(gate)) * up
        else:
            out = outs[0]
        if has_res:
            out = out + res_ref[:, :width]
        o_ref[:, :width] = out.astype(o_ref.dtype)
        if has_emit:
            _emit_norm_input(out, gain_ref, xg_ref, ssq_ref)

    @pl.when(j == 0)
    def _():
        convert_chunk(0)

    tn = o_ref.shape[1]
    for parity in (0, 1):
        @pl.when((j > 0) & (j < ncb) & (j % 2 == parity))
        def _():
            convert_chunk(parity)
            multiply(1 - parity, tn)

    @pl.when(j == ncb)
    def _():
        multiply((ncb - 1) % 2, last_width)


def ws_matmul(a_list, w_list, layer, res=None, *, out_dtype, tm, tn, col0=0, ncols=None, shift=0,
              transposed=False, swiglu=False, side=None, row_ssq=None, next_gain=None, eps=1e-6,
              name="ws_matmul"):
    n_a, n_w = len(a_list), len(w_list)
    m, ka = a_list[0].shape
    if transposed:
        _, n_total, k_total = w_list[0].shape
    else:
        _, k_total, n_total = w_list[0].shape
    ncols = n_total if ncols is None else ncols
    nm = m // tm
    kc = k_total // nm
    assert ka * n_a == k_total and m % tm == 0 and col0 % tn == 0 and n_w == (2 if swiglu else 1)
    assert kc * nm == k_total and kc % LANES == 0
    assert shift % SUBLANES == 0 and tn % shift == 0 if shift else True
    assert transposed or not shift
    assert ncols % tn == 0 or (col0 + ncols == n_total and not shift)
    assert ncols % LANES == 0
    ncb = pl.cdiv(ncols, tn)
    cb = col0 // tn

    row_blk = lambda j, i: jnp.where(j > 0, i, 0)
    out_col = lambda j: jnp.maximum(j - 1, 0)
    w_col = lambda j: cb + jnp.minimum(j, ncb - 1)
    w_chunk = lambda j, i: jnp.where(j < ncb, i, nm - 1)

    in_specs = [pl.BlockSpec((tm, ka), lambda j, i: (row_blk(j, i), 0)) for _ in a_list]
    if transposed:
        in_specs += [pl.BlockSpec((None, tn, kc), lambda j, i: (layer, w_col(j), w_chunk(j, i)))
                     for _ in w_list]
    else:
        in_specs += [pl.BlockSpec((None, kc, tn), lambda j, i: (layer, w_chunk(j, i), w_col(j)))
                     for _ in w_list]
    args = list(a_list) + list(w_list)
    if shift:
        per = tn // shift
        in_specs += [pl.BlockSpec((None, shift, kc),
                                  lambda j, i: (layer, (w_col(j) + 1) * per, w_chunk(j, i)))
                     for _ in w_list]
        args += list(w_list)
    if res is not None:
        in_specs.append(pl.BlockSpec((tm, tn), lambda j, i: (row_blk(j, i), out_col(j))))
        args.append(res)
    out_specs = [pl.BlockSpec((tm, tn), lambda j, i: (row_blk(j, i), out_col(j)))]
    out_shape = [jax.ShapeDtypeStruct((m, ncols), out_dtype)]
    side_bytes = 0
    if side is not None:
        _, srows, scols = side.shape
        bf16_rows = 2 * SUBLANES
        slab = next(s for s in range(bf16_rows, srows + 1, bf16_rows)
                    if srows % s == 0 and srows // s <= (ncb + 1) * nm)
        last = srows // slab - 1
        in_specs.append(pl.BlockSpec((None, slab, scols),
                                     lambda j, i: (layer, jnp.minimum(j * nm + i, last), 0)))
        args.append(side)
        out_specs.append(pl.BlockSpec((slab, scols), lambda j, i: (jnp.minimum(j * nm + i, last), 0)))
        out_shape.append(jax.ShapeDtypeStruct((srows, scols), BF16))
        side_bytes = slab * scols * 6
    extra_bytes = 0
    if row_ssq is not None:
        in_specs.append(pl.BlockSpec((tm, row_ssq.shape[1]), lambda j, i: (row_blk(j, i), 0)))
        args.append(row_ssq)
        extra_bytes += tm * row_ssq.shape[1] * 4
    if next_gain is not None:
        assert ncols % tn == 0
        in_specs.append(pl.BlockSpec((1, tn), lambda j, i: (0, out_col(j))))
        args.append(next_gain.reshape(1, ncols))
        out_specs.append(pl.BlockSpec((tm, tn), lambda j, i: (row_blk(j, i), out_col(j))))
        out_shape.append(jax.ShapeDtypeStruct((m, ncols), BF16))
        out_specs.append(pl.BlockSpec((tm, LANES), lambda j, i: (row_blk(j, i), out_col(j))))
        out_shape.append(jax.ShapeDtypeStruct((m, ncb * LANES), F32))
        extra_bytes += tm * tn * 2 + tm * LANES * 4
    osz = jnp.dtype(out_dtype).itemsize
    vmem = (2 * (n_a * tm * ka * 2 + n_w * kc * (tn + shift) * 4 + tm * tn * osz
                 + (tm * tn * 4 if res is not None else 0) + side_bytes + extra_bytes)
            + 2 * n_w * k_total * tn * 2 + (n_w + 1) * tm * tn * 4 + VMEM_SLACK)
    outs = pl.pallas_call(
        functools.partial(_ws_matmul_kernel, n_a=n_a, n_w=n_w, ncb=ncb, kc=kc,
                          last_width=ncols - (ncb - 1) * tn, transposed=transposed,
                          shift=shift, has_res=res is not None, has_side=side is not None,
                          has_scale=row_ssq is not None, has_emit=next_gain is not None,
                          swiglu=swiglu, inv_d=1.0 / k_total, eps=eps),
        grid=(ncb + 1, nm),
        in_specs=in_specs,
        out_specs=out_specs,
        out_shape=out_shape,
        scratch_shapes=[pltpu.VMEM((k_total, tn), BF16) for _ in range(2 * n_w)],
        compiler_params=_params(("arbitrary", "arbitrary"), vmem),
        name=name,
    )(*args)
    return outs[0] if len(outs) == 1 else tuple(outs)


def _softplus_parts(z):
    l = jnp.log(1.0 + jnp.exp(-jnp.abs(z)))
    return jnp.minimum(z, 0.0) - l, -jnp.maximum(z, 0.0) - l


def _split_bf16(x):
    hi = x.astype(BF16)
    lo = (x - hi.astype(F32)).astype(BF16)
    return hi, lo


def _dot_nt(a, b):
    return lax.dot_general(a, b, (((1,), (1,)), ((), ())), preferred_element_type=F32)


def _dot_tn(a, b):
    return lax.dot_general(a, b, (((0,), (0,)), ((), ())), preferred_element_type=F32)


def _iota2(shape, dim):
    return lax.broadcasted_iota(jnp.int32, shape, dim)


def _sb_kernel(q_ref, k_ref, v_ref, o_ref, *, tb, scale):
    i = pl.program_id(2)
    hd = SB_HEAD_DIM
    nh = q_ref.shape[1] // hd
    row = _iota2((tb, tb), 0)
    col = _iota2((tb, tb), 1)
    tri = jnp.where(row > col, 1.0, 0.0).astype(BF16)
    before = col < row
    qs = [q_ref[:, h * hd:(h + 1) * hd].astype(BF16) for h in range(nh)]

    def blocks(ks, carries, accs, diagonal):
        log_betas, log_rests, pieces = [], [], []
        for h in range(nh):
            kb = k_ref[pl.ds(ks, tb), h * hd:(h + 1) * hd].astype(BF16)
            log_beta, log_rest = _softplus_parts(_dot_nt(qs[h], kb) * scale)
            if diagonal:
                log_rest = jnp.where(before, log_rest, 0.0)
            log_betas.append(log_beta)
            log_rests.append(log_rest)
            pieces += list(_split_bf16(log_rest))
        tails = jnp.dot(jnp.concatenate(pieces, axis=0), tri, preferred_element_type=F32)
        new_carries, new_accs = [], []
        for h in range(nh):
            tail = tails[2 * h * tb:(2 * h + 1) * tb] + tails[(2 * h + 1) * tb:(2 * h + 2) * tb]
            w = jnp.exp(log_betas[h] + tail + carries[h])
            if diagonal:
                w = jnp.where(before, w, 0.0)
            vb = v_ref[pl.ds(ks, tb), h * hd:(h + 1) * hd].astype(BF16)
            new_accs.append(accs[h] + jnp.dot(w.astype(BF16), vb, preferred_element_type=F32))
            new_carries.append(carries[h] + jnp.sum(log_rests[h], axis=-1, keepdims=True))
        return tuple(new_carries), tuple(new_accs)

    def alive(carries):
        top = carries[0]
        for c in carries[1:]:
            top = jnp.maximum(top, c)
        return (jnp.max(top) >= SB_DEAD_LOG).astype(jnp.int32)

    k0 = pl.multiple_of(i * tb, tb)
    carries, accs = blocks(k0, (jnp.zeros((tb, 1), F32),) * nh, (jnp.zeros((tb, hd), F32),) * nh,
                           True)

    def cond(st):
        return (st[0] < i) & (st[1] > 0)

    def body(st):
        jb, _, carries, accs = st
        ks = pl.multiple_of((i - 1 - jb) * tb, tb)
        carries, accs = blocks(ks, carries, accs, False)
        return jb + 1, alive(carries), carries, accs

    _, _, _, accs = lax.while_loop(cond, body, (jnp.int32(0), alive(carries), carries, accs))
    for h in range(nh):
        o_ref[:, h * hd:(h + 1) * hd] = accs[h].astype(o_ref.dtype)


def stick_breaking(proj, bsz, t, *, tb=256, heads_per_step=4):
    m = proj.shape[0]
    nq = t // tb
    wd = heads_per_step * SB_HEAD_DIM
    cq, ck, cv = OFF_SQ // wd, OFF_SK // wd, OFF_SV // wd
    vmem = (2 * (tb * wd * 4 + 2 * t * wd * 4 + tb * wd * 2)
            + heads_per_step * 12 * tb * tb * 4 + VMEM_SLACK)
    return pl.pallas_call(
        functools.partial(_sb_kernel, tb=tb, scale=SB_HEAD_DIM ** -0.5),
        grid=(bsz, SB_HEADS // heads_per_step, nq),
        in_specs=[pl.BlockSpec((tb, wd), lambda b, h, i: (b * nq + i, cq + h)),
                  pl.BlockSpec((t, wd), lambda b, h, i: (b, ck + h)),
                  pl.BlockSpec((t, wd), lambda b, h, i: (b, cv + h))],
        out_specs=pl.BlockSpec((tb, wd), lambda b, h, i: (b * nq + i, h)),
        out_shape=jax.ShapeDtypeStruct((m, GROUP_WIDTH), BF16),
        compiler_params=_params(("parallel", "parallel", "arbitrary"), vmem),
        name="stick_breaking",
    )(proj, proj, proj)


def _gla_kernel(gq_ref, gk_ref, gv_ref, gg_ref, glr_ref, w2_ref, bg_ref, ng_ref, o_ref, st_ref,
                *, tt, eps):
    @pl.when(pl.program_id(1) == 0)
    def _():
        st_ref[...] = jnp.zeros_like(st_ref)

    row = _iota2((tt, tt), 0)
    col = _iota2((tt, tt), 1)
    cmr = col - row

    gate = jnp.dot(glr_ref[...].astype(BF16), w2_ref[...], preferred_element_type=F32) + bg_ref[...]
    log_alpha = _softplus_parts(gate)[0] * (1.0 / GLA_GATE_TAU)
    la_hi, la_lo = _split_bf16(log_alpha)

    def rowsum(sel):
        s = sel.astype(BF16)
        return (jnp.dot(s, la_hi, preferred_element_type=F32)
                + jnp.dot(s, la_lo, preferred_element_type=F32))

    cum = rowsum(jnp.where(col <= row, 1.0, 0.0))
    last = cum[tt - 1:tt, :]

    levels = []
    s = tt // 2
    while s >= GLA_DIAG:
        start_r = (row & ~(2 * s - 1)) + s
        is_r = (row & s) != 0
        sel = jnp.where(is_r,
                        jnp.where((col >= start_r) & (col <= row), 1.0, 0.0),
                        jnp.where((col > row) & (col < start_r), 1.0, 0.0))
        same_pair = (row & ~(2 * s - 1)) == (col & ~(2 * s - 1))
        levels.append((s, jnp.exp(rowsum(sel)), same_pair))
        s //= 2

    q_all = gq_ref[...] * (GLA_DK ** -0.5)
    k_all = gk_ref[...]
    v_all = gv_ref[...]
    rowl = _iota2((tt, GLA_DK), 0)
    dv = GLA_DV
    for h in range(GLA_HEADS):
        ksl = slice(h * GLA_DK, (h + 1) * GLA_DK)
        vsl = slice(h * dv, (h + 1) * dv)
        q, k, c = q_all[:, ksl], k_all[:, ksl], cum[:, ksl]
        v_bf = v_all[:, vsl].astype(BF16)
        st = st_ref[h]

        o = _dot_nt((q * jnp.exp(c)).astype(BF16), st.astype(BF16))

        scores = jnp.zeros((tt, tt), F32)
        for s, decay, same_pair in levels:
            d = decay[:, ksl]
            is_r = (rowl & s) != 0
            qh = jnp.where(is_r, q * d, 0.0).astype(BF16)
            kh = jnp.where(is_r, 0.0, k * d).astype(BF16)
            sc = _dot_nt(qh, kh)
            scores = scores + (sc if 2 * s == tt else jnp.where(same_pair, sc, 0.0))

        for dd in range(GLA_DIAG):
            k_sh = k if dd == 0 else pltpu.roll(k, dd, 0)
            c_sh = c if dd == 0 else pltpu.roll(c, dd, 0)
            e = jnp.exp(jnp.minimum(c - c_sh, 0.0))
            sd = jnp.sum(q * k_sh * e, axis=-1, keepdims=True)
            ok = (cmr == -dd) & ((row & (GLA_DIAG - 1)) >= dd)
            scores = scores + jnp.where(ok, sd, 0.0)

        o = o + jnp.dot(scores.astype(BF16), v_bf, preferred_element_type=F32)

        lh = last[:, ksl]
        k_dec = (k * jnp.exp(lh - c)).astype(BF16)
        st_ref[h] = st * jnp.exp(lh) + _dot_tn(v_bf, k_dec)

        ms = jnp.mean(o * o, axis=-1, keepdims=True)
        y = (o * lax.rsqrt(ms + eps)) * ng_ref[...]
        g_out = gg_ref[:, vsl]
        o_ref[:, vsl] = (y * (g_out * jax.nn.sigmoid(g_out))).astype(o_ref.dtype)


def gla(proj, glr, w2, b_gate, norm_g, bsz, t, *, tt=128, eps=1e-6):
    m = proj.shape[0]
    nt = t // tt
    hk = GLA_HEADS * GLA_DK
    row = lambda b, i: b * nt + i
    vmem = (2 * (2 * tt * hk * 4 + 2 * tt * GROUP_WIDTH * 4 + tt * LANES * 4 + tt * GROUP_WIDTH * 2)
            + GLA_HEADS * GLA_DV * GLA_DK * 4 + 24 * tt * hk * 4 + VMEM_SLACK)
    return pl.pallas_call(
        functools.partial(_gla_kernel, tt=tt, eps=eps),
        grid=(bsz, nt),
        in_specs=[pl.BlockSpec((tt, hk), lambda b, i: (row(b, i), OFF_GQ // hk)),
                  pl.BlockSpec((tt, hk), lambda b, i: (row(b, i), OFF_GK // hk)),
                  pl.BlockSpec((tt, GROUP_WIDTH), lambda b, i: (row(b, i), OFF_GV // GROUP_WIDTH)),
                  pl.BlockSpec((tt, GROUP_WIDTH), lambda b, i: (row(b, i), OFF_GG // GROUP_WIDTH)),
                  pl.BlockSpec((tt, LANES), lambda b, i: (row(b, i), 0)),
                  pl.BlockSpec((LANES, hk), lambda b, i: (0, 0)),
                  pl.BlockSpec((1, hk), lambda b, i: (0, 0)),
                  pl.BlockSpec((1, GLA_DV), lambda b, i: (0, 0))],
        out_specs=pl.BlockSpec((tt, GROUP_WIDTH), lambda b, i: (row(b, i), 0)),
        out_shape=jax.ShapeDtypeStruct((m, GROUP_WIDTH), BF16),
        scratch_shapes=[pltpu.VMEM((GLA_HEADS, GLA_DV, GLA_DK), F32)],
        compiler_params=_params(("parallel", "arbitrary"), vmem),
        name="gla",
    )(proj, proj, proj, proj, glr, w2, b_gate.reshape(1, hk), norm_g.reshape(1, GLA_DV))


def _lru_kernel(lx_ref, lg_ref, cw_ref, cb_ref, wax_ref, ba_ref, bx_ref, lam_ref, o_ref,
                ext_ref, h_ref, *, tt):
    pad = SUBLANES

    @pl.when(pl.program_id(1) == 0)
    def _():
        ext_ref[0:pad, :] = jnp.zeros((pad, ext_ref.shape[1]), F32)
        h_ref[...] = jnp.zeros_like(h_ref)

    ext_ref[pad:pad + tt, :] = lx_ref[...]
    xc = cb_ref[...] + cw_ref[0:1, :] * ext_ref[pl.ds(pad - LRU_CONV + 1, tt), :]
    for kk in range(1, LRU_CONV):
        xc = xc + cw_ref[kk:kk + 1, :] * ext_ref[pl.ds(pad - LRU_CONV + 1 + kk, tt), :]
    ext_ref[0:pad, :] = ext_ref[tt:tt + pad, :]

    bd = LRU_BLOCK_DIM
    r_parts, i_parts = [], []
    for n in range(LRU_BLOCKS):
        ri = jnp.dot(xc[:, n * bd:(n + 1) * bd].astype(BF16), wax_ref[n], preferred_element_type=F32)
        r_parts.append(ri[:, :bd])
        i_parts.append(ri[:, bd:])
    r = jax.nn.sigmoid(jnp.concatenate(r_parts, axis=1) + ba_ref[...])
    ig = jax.nn.sigmoid(jnp.concatenate(i_parts, axis=1) + bx_ref[...])

    lam = lam_ref[...]
    softplus_neg_lam = jnp.maximum(-lam, 0.0) + jnp.log1p(jnp.exp(-jnp.abs(lam)))
    log_a = (-LRU_C) * r * softplus_neg_lam
    a = jnp.exp(log_a)
    u = jnp.exp(0.5 * jnp.log(-jnp.tanh(log_a) * (a * a + 1.0))) * (ig * xc)

    rowi = _iota2(a.shape, 0)
    sft = 1
    while sft < tt:
        keep = rowi >= sft
        a_prev = jnp.where(keep, pltpu.roll(a, sft, 0), 1.0)
        u_prev = jnp.where(keep, pltpu.roll(u, sft, 0), 0.0)
        u = a * u_prev + u
        a = a * a_prev
        sft *= 2
    h = u + a * h_ref[0:1, :]
    h_ref[0:1, :] = h[tt - 1:tt, :]

    lg = lg_ref[...]
    gelu = 0.5 * lg * (1.0 + jnp.tanh(0.7978845608028654 * (lg + 0.044715 * (lg * lg * lg))))
    o_ref[...] = (h * gelu).astype(o_ref.dtype)


def rglru(proj, conv_w, conv_b, wax, b_a, b_x, lam, bsz, t, *, tt=256):
    m = proj.shape[0]
    nt = t // tt
    w = GROUP_WIDTH
    row = lambda b, i: b * nt + i
    vec = lambda: pl.BlockSpec((1, w), lambda b, i: (0, 0))
    vmem = 2 * (2 * tt * w * 4 + tt * w * 2) + 24 * tt * w * 4 + VMEM_SLACK
    return pl.pallas_call(
        functools.partial(_lru_kernel, tt=tt),
        grid=(bsz, nt),
        in_specs=[pl.BlockSpec((tt, w), lambda b, i: (row(b, i), OFF_LX // w)),
                  pl.BlockSpec((tt, w), lambda b, i: (row(b, i), OFF_LG // w)),
                  pl.BlockSpec((LRU_CONV, w), lambda b, i: (0, 0)),
                  vec(),
                  pl.BlockSpec((LRU_BLOCKS, LRU_BLOCK_DIM, 2 * LRU_BLOCK_DIM), lambda b, i: (0, 0, 0)),
                  vec(), vec(), vec()],
        out_specs=pl.BlockSpec((tt, w), lambda b, i: (row(b, i), 0)),
        out_shape=jax.ShapeDtypeStruct((m, w), BF16),
        scratch_shapes=[pltpu.VMEM((tt + SUBLANES, w), F32), pltpu.VMEM((SUBLANES, w), F32)],
        compiler_params=_params(("parallel", "arbitrary"), vmem),
        name="rglru",
    )(proj, proj, conv_w, conv_b.reshape(1, w), wax, b_a.reshape(1, w), b_x.reshape(1, w),
      lam.reshape(1, w))


CONF_PAD = 32


def _conformer_kernel(val_ref, gte_ref, cw_ref, cb_ref, g_ref, b_ref, o_ref, ext_ref, sh_ref,
                      *, tt, eps):
    pad = CONF_PAD
    rows = tt + pad

    @pl.when(pl.program_id(1) == 0)
    def _():
        ext_ref[0:pad, :] = jnp.zeros((pad, ext_ref.shape[1]), F32)

    ext_ref[pad:pad + tt, :] = val_ref[...] * jax.nn.sigmoid(gte_ref[...])
    ext = ext_ref[...]
    for b in range(1, SUBLANES):
        sh_ref[b - 1] = pltpu.roll(ext, rows - b, 0)
    base = pad - CONV_KERNEL + 1
    y = cb_ref[...]
    for kk in range(CONV_KERNEL):
        a, b = divmod(base + kk, SUBLANES)
        src = ext_ref if b == 0 else sh_ref.at[b - 1]
        y = y + cw_ref[kk:kk + 1, :] * src[pl.ds(a * SUBLANES, tt), :]
    ext_ref[0:pad, :] = ext_ref[tt:tt + pad, :]

    mu = jnp.mean(y, axis=-1, keepdims=True)
    yc = y - mu
    var = jnp.mean(yc * yc, axis=-1, keepdims=True)
    z = (yc * lax.rsqrt(var + eps)) * g_ref[...] + b_ref[...]
    o_ref[...] = (z * jax.nn.sigmoid(z)).astype(o_ref.dtype)


def conformer(proj, conv_w, conv_b, ln_g, ln_b, bsz, t, *, tt=256, eps=1e-5):
    m = proj.shape[0]
    nt = t // tt
    w = GROUP_WIDTH
    row = lambda b, i: b * nt + i
    vec = lambda: pl.BlockSpec((1, w), lambda b, i: (0, 0))
    vmem = (2 * (2 * tt * w * 4 + tt * w * 2) + (8 * tt + SUBLANES * (tt + CONF_PAD)) * w * 4
            + VMEM_SLACK)
    return pl.pallas_call(
        functools.partial(_conformer_kernel, tt=tt, eps=eps),
        grid=(bsz, nt),
        in_specs=[pl.BlockSpec((tt, w), lambda b, i: (row(b, i), OFF_CU // w)),
                  pl.BlockSpec((tt, w), lambda b, i: (row(b, i), OFF_CU // w + 1)),
                  pl.BlockSpec((CONV_KERNEL, w), lambda b, i: (0, 0)),
                  vec(), vec(), vec()],
        out_specs=pl.BlockSpec((tt, w), lambda b, i: (row(b, i), 0)),
        out_shape=jax.ShapeDtypeStruct((m, w), BF16),
        scratch_shapes=[pltpu.VMEM((tt + CONF_PAD, w), F32),
                        pltpu.VMEM((SUBLANES - 1, tt + CONF_PAD, w), F32)],
        compiler_params=_params(("parallel", "arbitrary"), vmem),
        name="conformer",
    )(proj, proj, conv_w, conv_b.reshape(1, w), ln_g.reshape(1, w), ln_b.reshape(1, w))


def kernel(x, norm_mix_g, w_in, gla_w_gate2, gla_b_gate, gla_norm_g, lru_conv_w, lru_conv_b, lru_w_a, lru_b_a, lru_w_x, lru_b_x, lru_lambda, conf_conv_w, conf_conv_b, conf_ln_g, conf_ln_b, w_out, norm_ffn_g, ffn_w_gate, ffn_w_up, ffn_w_down, final_norm_g):
    bsz, t, d = x.shape
    depth = w_in.shape[0]
    xf = x.reshape(bsz * t, d)
    w_in_t = jnp.swapaxes(w_in, 1, 2)
    h = rmsnorm(xf, norm_mix_g[0], BF16)
    ssq = None
    for l in range(depth):
        rank0 = N_PROJ_GLA
        w2 = jnp.pad(gla_w_gate2[l], ((0, LANES - GLA_GATE_RANK), (0, 0))).astype(BF16)
        wax = jnp.concatenate([lru_w_a[l], lru_w_x[l]], axis=-1).astype(BF16)

        proj_gla = ws_matmul([h], [w_in_t], l, out_dtype=F32, tm=512, tn=1024, ncols=N_PROJ_GLA,
                             transposed=True, row_ssq=ssq, name="in_proj_gla")
        proj = ws_matmul([h], [w_in_t], l, out_dtype=F32, tm=512, tn=1024, col0=rank0,
                         ncols=N_PROJ_REST, shift=GLA_GATE_RANK, transposed=True, row_ssq=ssq,
                         name="in_proj_rest")
        glr = ws_matmul([h], [w_in_t], l, out_dtype=F32, tm=1024, tn=LANES, col0=rank0, ncols=LANES,
                        transposed=True, row_ssq=ssq, name="in_proj_rank")
        o_a = gla(proj_gla, glr, w2, gla_b_gate[l], gla_norm_g[l], bsz, t)
        o_b = stick_breaking(proj, bsz, t)
        o_c = rglru(proj, lru_conv_w[l], lru_conv_b[l], wax, lru_b_a[l], lru_b_x[l], lru_lambda[l], bsz, t)
        o_d = conformer(proj, conf_conv_w[l], conf_conv_b[l], conf_ln_g[l], conf_ln_b[l], bsz, t)
        xf, h, ssq = ws_matmul([o_a, o_b, o_c, o_d], [w_out], l, xf, out_dtype=F32, tm=512, tn=1024,
                               next_gain=norm_ffn_g[l], name="out_proj")

        act, w_down = ws_matmul([h], [ffn_w_gate, ffn_w_up], l, out_dtype=BF16, tm=1024, tn=512,
                                swiglu=True, side=ffn_w_down, row_ssq=ssq, name="ffn_gate_up")
        if l + 1 < depth:
            xf, h, ssq = res_matmul(act, w_down, xf, tm=512, tn=512, next_gain=norm_mix_g[l + 1])
        else:
            xf = res_matmul(act, w_down, xf, tm=512, tn=512)
    return rmsnorm(xf, final_norm_g, F32).reshape(bsz, t, d)
```
